```python
import math
import jax, jax.numpy as jnp
from jax import lax
import numpy as np

D_MODEL = 2048
BATCH = 4
SEQ = 4096
DEPTH = 4

CTX_LEN = 256
GRID_W = 64
N_MIXERS = 3
RMS_EPS = 1e-6
F32 = jnp.float32

DN_DK = 128
DN_DV = 128
DN_HK = D_MODEL // 128
DN_HV = 2 * DN_HK
DN_QK = DN_HK * DN_DK
DN_VW = DN_HV * DN_DV
DN_CONV_CH = 2 * DN_QK + DN_VW
DN_CONV_K = 5
DN_CHUNK = 64
DN_PROJ = DN_CONV_CH + DN_VW + 4 * DN_HV

DA_DK = 128
DA_H = D_MODEL // (2 * DA_DK)
DA_QK = 2 * DA_H * DA_DK
DA_VW = DA_H * 2 * DA_DK
DA_PROJ = 2 * DA_QK + 2 * DA_VW
DA_BLOCK = 128
ROPE_THETA = 10000.0

NA_DH = 128
NA_H = D_MODEL // NA_DH
NA_W = NA_H * NA_DH
NA_PROJ = 4 * NA_W
NA_WR = 8
NA_WC = 16

kernel_name = 'hybrid_dit_deltanet_diffattn_natten'


def rmsnorm(x, w, eps=RMS_EPS):
    xf = x.astype(F32)
    y = xf * lax.rsqrt(jnp.mean(xf * xf, axis=-1, keepdims=True) + eps)
    return (y * w.astype(F32)).astype(x.dtype)


def l2norm(x):
    xf = x.astype(F32)
    return xf * lax.rsqrt(jnp.sum(xf * xf, axis=-1, keepdims=True) + 1e-6)


def rope_2d_tables(n, head_dim):
    t = jnp.arange(n)
    row = (t // GRID_W).astype(F32)
    col = (t % GRID_W).astype(F32)
    half = head_dim // 2
    inv = ROPE_THETA ** (-jnp.arange(0, half, 2, dtype=F32) / half)
    ang_r = row[:, None] * inv
    ang_c = col[:, None] * inv
    ang = jnp.concatenate([ang_r, ang_r, ang_c, ang_c], axis=-1)
    return jnp.cos(ang), jnp.sin(ang)


def apply_rope_2d(x, cos, sin):
    a1, a2, b1, b2 = jnp.split(x, 4, axis=-1)
    rot = jnp.concatenate([-a2, a1, -b2, b1], axis=-1)
    return (x * cos[:, None, :] + rot * sin[:, None, :]).astype(x.dtype)


def short_conv_silu(x, w):
    k = jnp.transpose(w)[:, None, :].astype(x.dtype)
    y = lax.conv_general_dilated(x, k, window_strides=(1,),
                                 padding=[(DN_CONV_K // 2, DN_CONV_K // 2)],
                                 dimension_numbers=('NWC', 'WIO', 'NWC'),
                                 feature_group_count=x.shape[-1])
    return jax.nn.silu(y)


def softmax_attend(q, k, v):
    s = jnp.einsum('bqhd,bkhd->bhqk', q, k, preferred_element_type=F32)
    p = jax.nn.softmax(s, axis=-1)
    return jnp.einsum('bhqk,bkhe->bqhe', p.astype(v.dtype), v, preferred_element_type=F32)


def chunk_gated_delta(q, k, v, g, beta, s0):
    B, n, H, dk = k.shape
    dv = v.shape[-1]
    C = DN_CHUNK
    nc = n // C

    def chunks(t):
        t = t.astype(F32).reshape((B, nc, C, H) + t.shape[3:])
        return jnp.moveaxis(t, (1, 3), (0, 2))

    qc = chunks(q) * (dk ** -0.5)
    kc = chunks(k)
    vc = chunks(v)
    bc = chunks(beta)
    gc = jnp.cumsum(chunks(g), axis=-1)
    idx = jnp.arange(C)
    incl = idx[:, None] >= idx[None, :]
    strict = idx[:, None] > idx[None, :]
    diff = gc[..., :, None] - gc[..., None, :]
    decay = jnp.where(incl, jnp.exp(jnp.where(incl, diff, 0.0)), 0.0)
    kb = kc * bc[..., None]
    lower = jnp.where(strict, jnp.einsum('nbhid,nbhjd->nbhij', kb, kc) * decay, 0.0)
    eye = jnp.eye(C, dtype=F32)
    tinv = lax.linalg.triangular_solve(eye + lower, jnp.broadcast_to(eye, lower.shape),
                                       left_side=True, lower=True, unit_diagonal=True)
    u = tinv @ (vc * bc[..., None])
    w = tinv @ (kb * jnp.exp(gc)[..., None])
    a_intra = jnp.einsum('nbhid,nbhjd->nbhij', qc, kc) * decay
    q_dec = qc * jnp.exp(gc)[..., None]
    k_dec = kc * jnp.exp(gc[..., -1:] - gc)[..., None]
    g_last = jnp.exp(gc[..., -1])

    def step(s, inp):
        q_i, k_i, u_i, w_i, a_i, gl_i = inp
        v_new = u_i - w_i @ s
        o_i = q_i @ s + a_i @ v_new
        s = s * gl_i[..., None, None] + jnp.einsum('bhck,bhcv->bhkv', k_i, v_new)
        return s, o_i

    s_final, o = lax.scan(step, s0.astype(F32), (q_dec, k_dec, u, w, a_intra, g_last))
    o = jnp.moveaxis(o, (0, 2), (1, 3)).reshape(B, n, H, dv)
    return o, s_final


def deltanet_mixer(h_lat, h_ctx, w_in, conv_w, a_log, dt_bias, onorm_w, w_out, need_ctx):
    B = h_lat.shape[0]
    rep = DN_HV // DN_HK

    def project(h):
        n = h.shape[1]
        qkv, z, ba = jnp.split(h @ w_in, [DN_CONV_CH, DN_CONV_CH + DN_VW], axis=-1)
        q, k, v = jnp.split(short_conv_silu(qkv, conv_w), [DN_QK, 2 * DN_QK], axis=-1)
        q = jnp.repeat(l2norm(q.reshape(B, n, DN_HK, DN_DK)), rep, axis=2)
        k = jnp.repeat(l2norm(k.reshape(B, n, DN_HK, DN_DK)), rep, axis=2)
        v = v.reshape(B, n, DN_HV, DN_DV)
        ba = ba.reshape(B, n, 2, 2, DN_HV).astype(F32)
        beta = jax.nn.sigmoid(ba[:, :, :, 0])
        g = -jnp.exp(a_log.astype(F32)) * jax.nn.softplus(ba[:, :, :, 1] + dt_bias.astype(F32))
        return q, k, v, z, g, beta

    ql, kl, vl, zl, gl, bl = project(h_lat)
    qc, kc, vc, zc, gc, bc = project(h_ctx)
    outs_lat = []
    outs_ctx = []
    for d in range(2):
        rev = (lambda t: jnp.flip(t, axis=1)) if d == 1 else (lambda t: t)
        s0 = jnp.zeros((B, DN_HV, DN_DK, DN_DV), F32)
        oc, s_ctx = chunk_gated_delta(rev(qc), rev(kc), rev(vc), rev(gc[:, :, d]), rev(bc[:, :, d]), s0)
        ol, _ = chunk_gated_delta(rev(ql), rev(kl), rev(vl), rev(gl[:, :, d]), rev(bl[:, :, d]), s_ctx)
        outs_lat.append(rev(ol))
        if need_ctx:
            outs_ctx.append(rev(oc))

    def finish(o, z):
        n = o.shape[1]
        zh = z.reshape(B, n, DN_HV, DN_DV).astype(F32)
        y = rmsnorm(o, onorm_w) * jax.nn.silu(zh)
        return y.reshape(B, n, DN_VW).astype(z.dtype) @ w_out

    out_lat = finish(outs_lat[0] + outs_lat[1], zl)
    out_ctx = finish(outs_ctx[0] + outs_ctx[1], zc) if need_ctx else None
    return out_lat, out_ctx


def diff_attn_mixer(h_lat, h_ctx, w_in, lam, subln_w, w_out, layer_idx, need_ctx):
    B, N, _ = h_lat.shape
    lambda_init = 0.8 - 0.6 * math.exp(-0.3 * layer_idx)
    lam = lam.astype(F32)
    lam_full = jnp.exp(jnp.sum(lam[0] * lam[1])) - jnp.exp(jnp.sum(lam[2] * lam[3])) + lambda_init
    cos, sin = rope_2d_tables(N, DA_DK)

    def project(h):
        n = h.shape[1]
        q, k, v, gate = jnp.split(h @ w_in, [DA_QK, 2 * DA_QK, 2 * DA_QK + DA_VW], axis=-1)
        return (q.reshape(B, n, 2 * DA_H, DA_DK), k.reshape(B, n, 2 * DA_H, DA_DK),
                v.reshape(B, n, DA_H, 2 * DA_DK), gate)

    def attend(q, k, v):
        s = jnp.einsum('bqhjd,bkhjd->bhjqk', q, k, preferred_element_type=F32) * (DA_DK ** -0.5)
        p = jax.nn.softmax(s, axis=-1)
        a = p[:, :, 0] - lam_full * p[:, :, 1]
        return jnp.einsum('bhqk,bkhe->bqhe', a.astype(v.dtype), v, preferred_element_type=F32)

    ql, kl, vl, gl = project(h_lat)
    qc, kc, vc, gc = project(h_ctx)
    ql = apply_rope_2d(ql, cos, sin).reshape(B, N, DA_H, 2, DA_DK)
    kl = apply_rope_2d(kl, cos, sin).reshape(B, N, DA_H, 2, DA_DK)
    qc = qc.reshape(B, -1, DA_H, 2, DA_DK)
    kc = kc.reshape(B, -1, DA_H, 2, DA_DK)
    k_all = jnp.concatenate([kc, kl], axis=1)
    v_all = jnp.concatenate([vc, vl], axis=1)
    q_blocks = jnp.moveaxis(ql.reshape(B, N // DA_BLOCK, DA_BLOCK, DA_H, 2, DA_DK), 1, 0)
    o_lat = lax.map(lambda qb: attend(qb, k_all, v_all), q_blocks)
    o_lat = jnp.moveaxis(o_lat, 0, 1).reshape(B, N, DA_H, 2 * DA_DK)

    def finish(o, gate):
        n = o.shape[1]
        y = rmsnorm(o, subln_w, 1e-5) * (1.0 - lambda_init)
        return (y.reshape(B, n, DA_VW).astype(gate.dtype) * jax.nn.silu(gate)) @ w_out

    out_lat = finish(o_lat, gl)
    out_ctx = finish(attend(qc, kc, vc), gc) if need_ctx else None
    return out_lat, out_ctx


def neighbourhood_mixer(h_lat, h_ctx, w_in, rpb, w_out, need_ctx):
    B, N, _ = h_lat.shape
    rows = N // GRID_W
    wr = min(NA_WR, rows)

    def project(h):
        n = h.shape[1]
        q, k, v, gate = jnp.split(h @ w_in, 4, axis=-1)
        heads = lambda t: t.reshape(B, n, NA_H, NA_DH)
        return heads(q) * (NA_DH ** -0.5), heads(k), heads(v), gate

    ql, kl, vl, gl = project(h_lat)
    qc, kc, vc, gc = project(h_ctx)
    kg = kl.reshape(B, rows, GRID_W, NA_H, NA_DH)
    vg = vl.reshape(B, rows, GRID_W, NA_H, NA_DH)
    qg = jnp.moveaxis(ql.reshape(B, rows, GRID_W, NA_H, NA_DH), 1, 0)
    cols = jnp.arange(GRID_W)
    col_start = jnp.clip(cols - NA_WC // 2, 0, GRID_W - NA_WC)
    col_idx = col_start[:, None] + jnp.arange(NA_WC)
    dc = col_idx - cols[:, None] + (NA_WC - 1)
    row_start = jnp.clip(jnp.arange(rows) - NA_WR // 2, 0, rows - wr)
    n_win = wr * NA_WC

    def row_block(args):
        r, q_r = args
        rs = row_start[r]
        kb = lax.dynamic_slice_in_dim(kg, rs, wr, axis=1)
        vb = lax.dynamic_slice_in_dim(vg, rs, wr, axis=1)
        kn = kb[:, :, col_idx]
        vn = vb[:, :, col_idx]
        s_win = jnp.einsum('bqhd,biqjhd->bhqij', q_r, kn, preferred_element_type=F32)
        dr = rs + jnp.arange(wr) - r + (NA_WR - 1)
        bias = rpb[:, dr[None, :, None], dc[:, None, :]]
        s_win = s_win + bias.astype(F32)[None]
        s_ctx = jnp.einsum('bqhd,bkhd->bhqk', q_r, kc, preferred_element_type=F32)
        s = jnp.concatenate([s_win.reshape(B, NA_H, GRID_W, n_win), s_ctx], axis=-1)
        p = jax.nn.softmax(s, axis=-1)
        p_win = p[..., :n_win].reshape(B, NA_H, GRID_W, wr, NA_WC).astype(vn.dtype)
        o = jnp.einsum('bhqij,biqjhd->bqhd', p_win, vn, preferred_element_type=F32)
        o = o + jnp.einsum('bhqk,bkhd->bqhd', p[..., n_win:].astype(vc.dtype), vc, preferred_element_type=F32)
        return o

    o = lax.map(row_block, (jnp.arange(rows), qg))
    o_lat = jnp.moveaxis(o, 0, 1).reshape(B, N, NA_W)

    def finish(o, gate):
        return (o.astype(gate.dtype) * jax.nn.silu(gate)) @ w_out

    out_lat = finish(o_lat, gl)
    out_ctx = finish(softmax_attend(qc, kc, vc).reshape(B, -1, NA_W), gc) if need_ctx else None
    return out_lat, out_ctx


def setup_inputs(seed: int = 0) -> dict:
    key = jax.random.key(seed)
    ks = iter(jax.random.split(key, 32))
    nrm = lambda shape, scale: jax.random.normal(next(ks), shape, F32) * scale
    n_dn = len(range(0, DEPTH, N_MIXERS))
    n_da = len(range(1, DEPTH, N_MIXERS))
    n_na = len(range(2, DEPTH, N_MIXERS))
    x = nrm((BATCH, SEQ, D_MODEL), 1.0)
    c = nrm((BATCH, D_MODEL), 1.0)
    ctx = nrm((BATCH, CTX_LEN, D_MODEL), 1.0)
    c_ctx = nrm((D_MODEL,), 1.0)
    norm_w = 1.0 + nrm((DEPTH, D_MODEL), 0.02)
    ada_w = nrm((DEPTH, D_MODEL, 3 * D_MODEL), D_MODEL ** -0.5)
    ada_b = nrm((DEPTH, 3 * D_MODEL), 0.02)
    dn_w_in = nrm((n_dn, D_MODEL, DN_PROJ), D_MODEL ** -0.5)
    dn_conv_w = nrm((n_dn, DN_CONV_CH, DN_CONV_K), DN_CONV_K ** -0.5)
    dn_a_log = jnp.log(jax.random.uniform(next(ks), (n_dn, 2, DN_HV), F32, 1.0, 16.0))
    dt = jnp.exp(jax.random.uniform(next(ks), (n_dn, 2, DN_HV), F32, math.log(1e-3), math.log(1e-1)))
    dn_dt_bias = dt + jnp.log(-jnp.expm1(-dt))
    dn_onorm_w = 1.0 + nrm((n_dn, DN_DV), 0.02)
    dn_w_out = nrm((n_dn, DN_VW, D_MODEL), DN_VW ** -0.5)
    da_w_in = nrm((n_da, D_MODEL, DA_PROJ), D_MODEL ** -0.5)
    da_lambda = nrm((n_da, 4, DA_DK), 0.1)
    da_subln_w = 1.0 + nrm((n_da, 2 * DA_DK), 0.02)
    da_w_out = nrm((n_da, DA_VW, D_MODEL), DA_VW ** -0.5)
    na_w_in = nrm((n_na, D_MODEL, NA_PROJ), D_MODEL ** -0.5)
    na_rpb = nrm((n_na, NA_H, 2 * NA_WR - 1, 2 * NA_WC - 1), 0.1)
    na_w_out = nrm((n_na, NA_W, D_MODEL), NA_W ** -0.5)
    final_norm_w = 1.0 + nrm((D_MODEL,), 0.02)
    return {'x': x, 'c': c, 'ctx': ctx, 'c_ctx': c_ctx,
            'norm_w': norm_w, 'ada_w': ada_w, 'ada_b': ada_b,
            'dn_w_in': dn_w_in, 'dn_conv_w': dn_conv_w, 'dn_a_log': dn_a_log, 'dn_dt_bias': dn_dt_bias,
            'dn_onorm_w': dn_onorm_w, 'dn_w_out': dn_w_out,
            'da_w_in': da_w_in, 'da_lambda': da_lambda, 'da_subln_w': da_subln_w, 'da_w_out': da_w_out,
            'na_w_in': na_w_in, 'na_rpb': na_rpb, 'na_w_out': na_w_out,
            'final_norm_w': final_norm_w}


def reference(x, c, ctx, c_ctx, norm_w, ada_w, ada_b,
              dn_w_in, dn_conv_w, dn_a_log, dn_dt_bias, dn_onorm_w, dn_w_out,
              da_w_in, da_lambda, da_subln_w, da_w_out,
              na_w_in, na_rpb, na_w_out, final_norm_w):
    silu_c = jax.nn.silu(c)
    silu_cc = jax.nn.silu(c_ctx)
    for i in range(DEPTH):
        kind, j = i % N_MIXERS, i // N_MIXERS
        need_ctx = i < DEPTH - 1
        mod_l = silu_c @ ada_w[i] + ada_b[i]
        mod_c = silu_cc @ ada_w[i] + ada_b[i]
        sh_l, sc_l, gt_l = jnp.split(mod_l[:, None, :], 3, axis=-1)
        sh_c, sc_c, gt_c = jnp.split(mod_c, 3, axis=-1)
        h_lat = rmsnorm(x, norm_w[i]) * (1.0 + sc_l) + sh_l
        h_ctx = rmsnorm(ctx, norm_w[i]) * (1.0 + sc_c) + sh_c
        if kind == 0:
            out_lat, out_ctx = deltanet_mixer(h_lat, h_ctx, dn_w_in[j], dn_conv_w[j], dn_a_log[j],
                                              dn_dt_bias[j], dn_onorm_w[j], dn_w_out[j], need_ctx)
        elif kind == 1:
            out_lat, out_ctx = diff_attn_mixer(h_lat, h_ctx, da_w_in[j], da_lambda[j], da_subln_w[j],
                                               da_w_out[j], i, need_ctx)
        else:
            out_lat, out_ctx = neighbourhood_mixer(h_lat, h_ctx, na_w_in[j], na_rpb[j], na_w_out[j], need_ctx)
        x = x + gt_l * out_lat.astype(x.dtype)
        if need_ctx:
            ctx = ctx + gt_c * out_ctx.astype(ctx.dtype)
    return rmsnorm(x, final_norm_w)
```

```python
import functools
import math

import jax
import jax.numpy as jnp
from jax import lax
from jax.experimental import pallas as pl
from jax.experimental.pallas import tpu as pltpu

F32 = jnp.float32
BF16 = jnp.bfloat16

GRID_W = 64
N_MIXERS = 3
RMS_EPS = 1e-6
HEAD_DIM = 128
DN_CONV_K = 5
DN_CHUNK = 64
DN_HEADS_PER_STEP = 8
DA_SUBLN_EPS = 1e-5
ROPE_THETA = 10000.0
NA_WR = 8
NA_WC = 16
NEG_BIG = -1e30
MOD_ROWS = 8
V7X_VMEM_LIMIT = 56 * 1024 * 1024


def _sigmoid(x):
    return 1.0 / (1.0 + jnp.exp(-x))


def _silu(x):
    return x * _sigmoid(x)


def _dot(a, b):
    return jnp.dot(a, b, preferred_element_type=F32)


def _dot_nt(a, b):
    return lax.dot_general(a, b, (((1,), (1,)), ((), ())), preferred_element_type=F32)


def _dot_tn(a, b):
    return lax.dot_general(a, b, (((0,), (0,)), ((), ())), preferred_element_type=F32)


def _params(semantics, vmem=V7X_VMEM_LIMIT):
    return pltpu.CompilerParams(dimension_semantics=semantics, vmem_limit_bytes=vmem)


def _mod_kernel(c_ref, w_ref, b_ref, o_ref):
    s = _silu(c_ref[...])
    o_ref[0] = jnp.dot(s, w_ref[0], preferred_element_type=F32,
                       precision=lax.Precision.HIGHEST) + b_ref[0]


def _modulation(cvec, ada_w, ada_b):
    depth, d, p = ada_w.shape
    tn = 1024 if p % 1024 == 0 else p
    return pl.pallas_call(
        _mod_kernel,
        grid=(depth, p // tn),
        in_specs=[pl.BlockSpec((MOD_ROWS, d), lambda i, n: (0, 0)),
                  pl.BlockSpec((1, d, tn), lambda i, n: (i, 0, n)),
                  pl.BlockSpec((1, 1, tn), lambda i, n: (i, 0, n))],
        out_specs=pl.BlockSpec((1, MOD_ROWS, tn), lambda i, n: (i, 0, n)),
        out_shape=jax.ShapeDtypeStruct((depth, MOD_ROWS, p), F32),
        compiler_params=_params(("parallel", "parallel")),
        name="adaln_mod",
    )(cvec, ada_w, ada_b.reshape(depth, 1, p))


def _norm_proj_kernel(*refs, ctx_len, tm, has_extra):
    if has_extra:
        x_ref, nw_ref, scl_ref, shl_ref, scc_ref, shc_ref, w_ref, wx_ref, o_ref, ox_ref, h_ref = refs
    else:
        x_ref, nw_ref, scl_ref, shl_ref, scc_ref, shc_ref, w_ref, o_ref, h_ref = refs
    mi = pl.program_id(1)
    ni = pl.program_id(2)

    @pl.when(ni == 0)
    def _():
        x = x_ref[0]
        ms = jnp.mean(x * x, axis=-1, keepdims=True)
        y = x * lax.rsqrt(ms + RMS_EPS) * nw_ref[...]
        row = lax.broadcasted_iota(jnp.int32, (tm, 1), 0) + mi * tm
        is_ctx = row < ctx_len
        sc = jnp.where(is_ctx, scc_ref[...], scl_ref[0])
        sh = jnp.where(is_ctx, shc_ref[...], shl_ref[0])
        h = (y * (1.0 + sc) + sh).astype(BF16)
        h_ref[...] = h
        if has_extra:
            ox_ref[0] = _dot(h, wx_ref[...])

    o_ref[0] = _dot(h_ref[...], w_ref[...]).astype(BF16)


def _norm_proj(xa, nw, scl, shl, scc, shc, w, wx, ctx_len, tm):
    b, t, d = xa.shape
    p = w.shape[1]
    tn = 512
    assert t % tm == 0 and p % tn == 0
    has_extra = wx is not None
    vec_l = pl.BlockSpec((1, 1, d), lambda bi, mi, ni: (bi, 0, 0))
    vec_c = pl.BlockSpec((1, d), lambda bi, mi, ni: (0, 0))
    in_specs = [pl.BlockSpec((1, tm, d), lambda bi, mi, ni: (bi, mi, 0)),
                vec_c, vec_l, vec_l, vec_c, vec_c,
                pl.BlockSpec((d, tn), lambda bi, mi, ni: (0, ni))]
    out_specs = [pl.BlockSpec((1, tm, tn), lambda bi, mi, ni: (bi, mi, ni))]
    out_shape = [jax.ShapeDtypeStruct((b, t, p), BF16)]
    args = [xa, nw.reshape(1, d), scl, shl, scc, shc, w]
    if has_extra:
        px = wx.shape[1]
        in_specs.append(pl.BlockSpec((d, px), lambda bi, mi, ni: (0, 0)))
        out_specs.append(pl.BlockSpec((1, tm, px), lambda bi, mi, ni: (bi, mi, 0)))
        out_shape.append(jax.ShapeDtypeStruct((b, t, px), F32))
        args.append(wx)
    outs = pl.pallas_call(
        functools.partial(_norm_proj_kernel, ctx_len=ctx_len, tm=tm, has_extra=has_extra),
        grid=(b, t // tm, p // tn),
        in_specs=in_specs,
        out_specs=out_specs,
        out_shape=out_shape,
        scratch_shapes=[pltpu.VMEM((tm, d), BF16)],
        compiler_params=_params(("parallel", "parallel", "arbitrary")),
        name="norm_proj",
    )(*args)
    return (outs[0], outs[1]) if has_extra else (outs[0], None)


def _out_proj_kernel(y_ref, w_ref, x_ref, gtl_ref, gtc_ref, o_ref, *, ctx_len, tm):
    mi = pl.program_id(1)
    acc = _dot(y_ref[0], w_ref[...])
    row = lax.broadcasted_iota(jnp.int32, (tm, 1), 0) + mi * tm
    gt = jnp.where(row < ctx_len, gtc_ref[...], gtl_ref[0])
    o_ref[0] = x_ref[0] + gt * acc


def _out_proj(y, w, xa, gtl, gtc, ctx_len, tm):
    b, t, d = xa.shape
    kd = y.shape[2]
    tn = 512
    assert d % tn == 0 and t % tm == 0
    return pl.pallas_call(
        functools.partial(_out_proj_kernel, ctx_len=ctx_len, tm=tm),
        grid=(b, t // tm, d // tn),
        in_specs=[pl.BlockSpec((1, tm, kd), lambda bi, mi, ni: (bi, mi, 0)),
                  pl.BlockSpec((kd, tn), lambda bi, mi, ni: (0, ni)),
                  pl.BlockSpec((1, tm, tn), lambda bi, mi, ni: (bi, mi, ni)),
                  pl.BlockSpec((1, 1, tn), lambda bi, mi, ni: (bi, 0, ni)),
                  pl.BlockSpec((1, tn), lambda bi, mi, ni: (0, ni))],
        out_specs=pl.BlockSpec((1, tm, tn), lambda bi, mi, ni: (bi, mi, ni)),
        out_shape=jax.ShapeDtypeStruct((b, t, d), F32),
        compiler_params=_params(("parallel", "parallel", "arbitrary")),
        name="out_proj",
    )(y, w, xa, gtl, gtc)


def _final_norm_kernel(x_ref, w_ref, o_ref):
    x = x_ref[0]
    ms = jnp.mean(x * x, axis=-1, keepdims=True)
    o_ref[0] = x * lax.rsqrt(ms + RMS_EPS) * w_ref[...]


def _final_norm(xa, w, ctx_len):
    b, t, d = xa.shape
    n = t - ctx_len
    tr = math.gcd(ctx_len, 256)
    assert n % tr == 0
    off = ctx_len // tr
    return pl.pallas_call(
        _final_norm_kernel,
        grid=(b, n // tr),
        in_specs=[pl.BlockSpec((1, tr, d), lambda bi, ri: (bi, ri + off, 0)),
                  pl.BlockSpec((1, d), lambda bi, ri: (0, 0))],
        out_specs=pl.BlockSpec((1, tr, d), lambda bi, ri: (bi, ri, 0)),
        out_shape=jax.ShapeDtypeStruct((b, n, d), F32),
        compiler_params=_params(("parallel", "parallel")),
        name="final_norm",
    )(xa, w.reshape(1, d))


def _dn_conv_kernel(x_ref, w_ref, o_ref, *, ctx_len, n_qk_blocks):
    blk = pl.program_id(1)
    x = x_ref[0].astype(F32)
    t = x.shape[0]
    w = w_ref[...]
    row = lax.broadcasted_iota(jnp.int32, x.shape, 0)
    seg = row >= ctx_len
    half = DN_CONV_K // 2
    acc = x * w[half:half + 1, :]
    for tap in range(DN_CONV_K):
        dlt = tap - half
        if dlt == 0:
            continue
        shifted = pltpu.roll(x, (-dlt) % t, 0)
        src = row + dlt
        ok = (src >= 0) & (src < t) & ((src >= ctx_len) == seg)
        acc = acc + jnp.where(ok, shifted, 0.0) * w[tap:tap + 1, :]
    y = _silu(acc)
    inv = lax.rsqrt(jnp.sum(y * y, axis=-1, keepdims=True) + 1e-6)
    y = y * jnp.where(blk < n_qk_blocks, inv, 1.0)
    o_ref[0] = y.astype(BF16)


def _dn_conv(proj, conv_w_t, ctx_len, conv_ch, n_qk_blocks):
    b, t, _ = proj.shape
    return pl.pallas_call(
        functools.partial(_dn_conv_kernel, ctx_len=ctx_len, n_qk_blocks=n_qk_blocks),
        grid=(b, conv_ch // HEAD_DIM),
        in_specs=[pl.BlockSpec((1, t, HEAD_DIM), lambda bi, ci: (bi, 0, ci)),
                  pl.BlockSpec((DN_CONV_K, HEAD_DIM), lambda bi, ci: (0, ci))],
        out_specs=pl.BlockSpec((1, t, HEAD_DIM), lambda bi, ci: (bi, 0, ci)),
        out_shape=jax.ShapeDtypeStruct((b, t, conv_ch), BF16),
        compiler_params=_params(("parallel", "parallel")),
        name="dn_conv",
    )(proj, conv_w_t)


def _exact_f32_dot(tri, g):
    hi = g.astype(BF16)
    r1 = g - hi.astype(F32)
    mid = r1.astype(BF16)
    lo = (r1 - mid.astype(F32)).astype(BF16)
    return _dot(tri, hi) + _dot(tri, mid) + _dot(tri, lo)


def _dn_gate_kernel(ba_ref, alog_ref, dtb_ref, beta_ref, gc_ref, *, hv):
    ba = ba_ref[0]
    z = ba + dtb_ref[...]
    softplus = jnp.maximum(z, 0.0) + jnp.log1p(jnp.exp(-jnp.abs(z)))
    g = -jnp.exp(alog_ref[...]) * softplus
    beta = _sigmoid(ba)
    ii = lax.broadcasted_iota(jnp.int32, (DN_CHUNK, DN_CHUNK), 0)
    jj = lax.broadcasted_iota(jnp.int32, (DN_CHUNK, DN_CHUNK), 1)
    lower = jnp.where(ii >= jj, 1.0, 0.0).astype(BF16)
    upper = jnp.where(ii <= jj, 1.0, 0.0).astype(BF16)
    gc_fwd = _exact_f32_dot(lower, g)
    gc_bwd = _exact_f32_dot(upper, g)
    beta_ref[0, 0] = beta[:, 0:hv]
    beta_ref[0, 1] = beta[:, 2 * hv:3 * hv]
    gc_ref[0, 0] = gc_fwd[:, hv:2 * hv]
    gc_ref[0, 1] = gc_bwd[:, 3 * hv:4 * hv]


def _dn_gates(ba, alog_row, dtb_row, hv):
    b, t, pw = ba.shape
    out = jax.ShapeDtypeStruct((b, 2, t, hv), F32)
    ospec = pl.BlockSpec((1, 2, DN_CHUNK, hv), lambda bi, ci: (bi, 0, ci, 0))
    return pl.pallas_call(
        functools.partial(_dn_gate_kernel, hv=hv),
        grid=(b, t // DN_CHUNK),
        in_specs=[pl.BlockSpec((1, DN_CHUNK, pw), lambda bi, ci: (bi, ci, 0)),
                  pl.BlockSpec((1, pw), lambda bi, ci: (0, 0)),
                  pl.BlockSpec((1, pw), lambda bi, ci: (0, 0))],
        out_specs=[ospec, ospec],
        out_shape=[out, out],
        compiler_params=_params(("parallel", "parallel")),
        name="dn_gates",
    )(ba, alog_row, dtb_row)


def _dn_chunk_kernel(q_ref, k_ref, v_ref, beta_ref, gcc_ref, gcr_ref, o_ref, s_ref, *, heads, scale):
    d = pl.program_id(2)
    p = pl.program_id(3)

    @pl.when(p == 0)
    def _():
        s_ref[...] = jnp.zeros_like(s_ref)

    c = DN_CHUNK
    ii = lax.broadcasted_iota(jnp.int32, (c, c), 0)
    jj = lax.broadcasted_iota(jnp.int32, (c, c), 1)
    order = (ii - jj) * (1 - 2 * d)
    incl = order >= 0
    strict = order > 0
    eye = jnp.where(ii == jj, 1.0, 0.0)
    pair_masks = []
    sz = 1
    while sz < c:
        pair_masks.append((ii // (2 * sz) == jj // (2 * sz)) & (ii // sz != jj // sz))
        sz *= 2
    fwd = d == 0
    for m in range(heads // 2):
        lanes = slice(m * HEAD_DIM, (m + 1) * HEAD_DIM)
        q = q_ref[0, :, lanes]
        k = k_ref[0, :, lanes]
        qf = q.astype(F32)
        kf = k.astype(F32)
        kk = _dot_nt(k, k)
        qk = _dot_nt(q, k) * scale
        for hh in range(2):
            h = 2 * m + hh
            hl = slice(h * HEAD_DIM, (h + 1) * HEAD_DIM)
            beta = beta_ref[0, 0, 0, :, h:h + 1]
            gcol = gcc_ref[0, 0, 0, :, h:h + 1]
            grow = gcr_ref[0, 0, 0, 0, h:h + 1, :]
            gam = jnp.where(incl, jnp.exp(jnp.where(incl, gcol - grow, 0.0)), 0.0)
            low = jnp.where(strict, beta * kk * gam, 0.0)
            tinv = eye - jnp.where(pair_masks[0], low, 0.0)
            for pm in pair_masks[1:]:
                tb = tinv.astype(BF16)
                xc = _dot(tb, jnp.where(pm, low, 0.0).astype(BF16))
                tinv = tinv - _dot(xc.astype(BF16), tb)
            tb = tinv.astype(BF16)
            eg = jnp.exp(gcol)
            glast = jnp.where(fwd, gcol[c - 1:c, :], gcol[0:1, :])
            vf = v_ref[0, :, hl].astype(F32)
            u = _dot(tb, (beta * vf).astype(BF16))
            w = _dot(tb, (beta * eg * kf).astype(BF16))
            s = s_ref[h]
            sb = s.astype(BF16)
            v_new = u - _dot(w.astype(BF16), sb)
            vnb = v_new.astype(BF16)
            o = _dot((qf * (scale * eg)).astype(BF16), sb) + _dot((qk * gam).astype(BF16), vnb)
            k_dec = (kf * jnp.exp(glast - gcol)).astype(BF16)
            s_ref[h] = s * jnp.exp(glast) + _dot_tn(k_dec, vnb)
            o_ref[0, 0, :, hl] = o.astype(BF16)


def _dn_chunks(qkv, beta_g, gcc_g, gcr_g, ctx_len, qk_w, hv):
    b, t, _ = qkv.shape
    g = DN_HEADS_PER_STEP
    assert hv % g == 0 and qk_w % (g // 2 * HEAD_DIM) == 0
    nc = t // DN_CHUNK
    nctx = ctx_len // DN_CHUNK
    qw = g // 2 * HEAD_DIM
    vw = g * HEAD_DIM
    k_off = qk_w // qw
    v_off = 2 * qk_w // vw

    def chunk(d, p):
        back = jnp.where(p < nctx, nctx - 1 - p, nc + nctx - 1 - p)
        return jnp.where(d == 0, p, back)

    return pl.pallas_call(
        functools.partial(_dn_chunk_kernel, heads=g, scale=HEAD_DIM ** -0.5),
        grid=(b, hv // g, 2, nc),
        in_specs=[pl.BlockSpec((1, DN_CHUNK, qw), lambda bi, gi, d, p: (bi, chunk(d, p), gi)),
                  pl.BlockSpec((1, DN_CHUNK, qw), lambda bi, gi, d, p: (bi, chunk(d, p), k_off + gi)),
                  pl.BlockSpec((1, DN_CHUNK, vw), lambda bi, gi, d, p: (bi, chunk(d, p), v_off + gi)),
                  pl.BlockSpec((1, 1, 1, DN_CHUNK, g), lambda bi, gi, d, p: (bi, d, gi, chunk(d, p), 0)),
                  pl.BlockSpec((1, 1, 1, DN_CHUNK, g), lambda bi, gi, d, p: (bi, d, gi, chunk(d, p), 0)),
                  pl.BlockSpec((1, 1, 1, 1, g, DN_CHUNK), lambda bi, gi, d, p: (bi, d, gi, chunk(d, p), 0, 0))],
        out_specs=pl.BlockSpec((1, 1, DN_CHUNK, vw), lambda bi, gi, d, p: (bi, d, chunk(d, p), gi)),
        out_shape=jax.ShapeDtypeStruct((b, 2, t, hv * HEAD_DIM), BF16),
        scratch_shapes=[pltpu.VMEM((g, HEAD_DIM, HEAD_DIM), F32)],
        compiler_params=_params(("parallel", "parallel", "arbitrary", "arbitrary")),
        name="dn_chunks",
    )(qkv, qkv, qkv, beta_g, gcc_g, gcr_g)


def _dn_finish_kernel(o_ref, z_ref, w_ref, y_ref):
    cw = y_ref.shape[2]
    for h in range(cw // HEAD_DIM):
        hl = slice(h * HEAD_DIM, (h + 1) * HEAD_DIM)
        o = o_ref[0, 0, :, hl].astype(F32) + o_ref[0, 1, :, hl].astype(F32)
        ms = jnp.mean(o * o, axis=-1, keepdims=True)
        y = o * lax.rsqrt(ms + RMS_EPS) * w_ref[...]
        y_ref[0, :, hl] = (y * _silu(z_ref[0, :, hl].astype(F32))).astype(BF16)


def _dn_finish(o, proj, onorm_w, z_off):
    b, _, t, vw = o.shape
    cw = 1024 if vw % 1024 == 0 else vw
    assert z_off % cw == 0
    tr = math.gcd(t, 256)
    return pl.pallas_call(
        _dn_finish_kernel,
        grid=(b, t // tr, vw // cw),
        in_specs=[pl.BlockSpec((1, 2, tr, cw), lambda bi, ri, ci: (bi, 0, ri, ci)),
                  pl.BlockSpec((1, tr, cw), lambda bi, ri, ci: (bi, ri, z_off // cw + ci)),
                  pl.BlockSpec((1, HEAD_DIM), lambda bi, ri, ci: (0, 0))],
        out_specs=pl.BlockSpec((1, tr, cw), lambda bi, ri, ci: (bi, ri, ci)),
        out_shape=jax.ShapeDtypeStruct((b, t, vw), BF16),
        compiler_params=_params(("parallel", "parallel", "parallel")),
        name="dn_finish",
    )(o, proj, onorm_w.reshape(1, HEAD_DIM))


def _deltanet(proj, ba, conv_w, a_log, dt_bias, onorm_w, ctx_len, d_model):
    hk = d_model // HEAD_DIM
    hv = 2 * hk
    qk_w = hk * HEAD_DIM
    conv_ch = 2 * qk_w + hv * HEAD_DIM
    b, t, _ = proj.shape
    g = DN_HEADS_PER_STEP
    qkv = _dn_conv(proj, jnp.transpose(conv_w), ctx_len, conv_ch, 2 * hk)
    zeros = jnp.zeros((hv,), F32)
    alog_row = jnp.concatenate([zeros, a_log[0], zeros, a_log[1]]).reshape(1, 4 * hv)
    dtb_row = jnp.concatenate([zeros, dt_bias[0], zeros, dt_bias[1]]).reshape(1, 4 * hv)
    beta, gc = _dn_gates(ba, alog_row, dtb_row, hv)
    nc = t // DN_CHUNK
    beta_g = beta.reshape(b, 2, t, hv // g, g).transpose(0, 1, 3, 2, 4)
    gcc_g = gc.reshape(b, 2, t, hv // g, g).transpose(0, 1, 3, 2, 4)
    gcr_g = gc.reshape(b, 2, nc, DN_CHUNK, hv // g, g).transpose(0, 1, 4, 2, 5, 3)
    o = _dn_chunks(qkv, beta_g, gcc_g, gcr_g, ctx_len, qk_w, hv)
    return _dn_finish(o, proj, onorm_w, conv_ch)


def _rope(x, cos, sin):
    quarter = HEAD_DIM // 4
    lane = lax.broadcasted_iota(jnp.int32, x.shape, 1)
    up = pltpu.roll(x, HEAD_DIM - quarter, 1)
    down = pltpu.roll(x, quarter, 1)
    rot = jnp.where((lane % (2 * quarter)) < quarter, -up, down)
    return x * cos + rot * sin


def _da_kernel(q_ref, k_ref, v_ref, g_ref, cos_ref, sin_ref, lam_ref, sw_ref, o_ref, krot_ref,
               *, ctx_len, lambda_init, scale):
    qi = pl.program_id(2)
    t = k_ref.shape[1]
    tq = q_ref.shape[1]

    @pl.when(qi == 0)
    def _():
        for j in range(2):
            lanes = slice(j * HEAD_DIM, (j + 1) * HEAD_DIM)
            kk = k_ref[0, :, lanes].astype(F32)
            krot_ref[:, lanes] = _rope(kk, cos_ref[...], sin_ref[...]).astype(BF16)

    lam = lam_ref[...]
    lam_full = (jnp.exp(jnp.sum(lam[0:1] * lam[1:2], axis=-1, keepdims=True))
                - jnp.exp(jnp.sum(lam[2:3] * lam[3:4], axis=-1, keepdims=True)) + lambda_init)
    qs = pl.multiple_of(qi * tq, tq)
    cos_q = cos_ref[pl.ds(qs, tq), :]
    sin_q = sin_ref[pl.ds(qs, tq), :]

    def attend(nk):
        outs = []
        for j in range(2):
            lanes = slice(j * HEAD_DIM, (j + 1) * HEAD_DIM)
            q = _rope(q_ref[0, :, lanes].astype(F32), cos_q, sin_q).astype(BF16)
            s = _dot_nt(q, krot_ref[0:nk, lanes]) * scale
            m = jnp.max(s, axis=-1, keepdims=True)
            pr = jnp.exp(s - m)
            l = jnp.sum(pr, axis=-1, keepdims=True)
            outs.append(_dot(pr.astype(BF16), v_ref[0, 0:nk, :]) / l)
        o = outs[0] - lam_full * outs[1]
        ms = jnp.mean(o * o, axis=-1, keepdims=True)
        y = o * lax.rsqrt(ms + DA_SUBLN_EPS) * sw_ref[...] * (1.0 - lambda_init)
        o_ref[0] = (y * _silu(g_ref[0].astype(F32))).astype(BF16)

    @pl.when(qi == 0)
    def _():
        attend(ctx_len)

    @pl.when(qi > 0)
    def _():
        attend(t)


def _rope_tables(ctx_len, n):
    tok = jnp.arange(n)
    row = (tok // GRID_W).astype(F32)
    col = (tok % GRID_W).astype(F32)
    half = HEAD_DIM // 2
    inv = ROPE_THETA ** (-jnp.arange(0, half, 2, dtype=F32) / half)
    ang_r = row[:, None] * inv
    ang_c = col[:, None] * inv
    ang = jnp.concatenate([ang_r, ang_r, ang_c, ang_c], axis=-1)
    cos = jnp.concatenate([jnp.ones((ctx_len, HEAD_DIM), F32), jnp.cos(ang)], axis=0)
    sin = jnp.concatenate([jnp.zeros((ctx_len, HEAD_DIM), F32), jnp.sin(ang)], axis=0)
    return cos, sin


def _diff_attn(proj, lam, subln_w, ctx_len, d_model, layer_idx):
    b, t, _ = proj.shape
    hw = 2 * HEAD_DIM
    heads = d_model // hw
    tq = ctx_len
    assert t % tq == 0
    lambda_init = 0.8 - 0.6 * math.exp(-0.3 * layer_idx)
    cos, sin = _rope_tables(ctx_len, t - ctx_len)
    full = lambda bi, hi, qi: (0, 0)
    return pl.pallas_call(
        functools.partial(_da_kernel, ctx_len=ctx_len, lambda_init=lambda_init, scale=HEAD_DIM ** -0.5),
        grid=(b, heads, t // tq),
        in_specs=[pl.BlockSpec((1, tq, hw), lambda bi, hi, qi: (bi, qi, hi)),
                  pl.BlockSpec((1, t, hw), lambda bi, hi, qi: (bi, 0, heads + hi)),
                  pl.BlockSpec((1, t, hw), lambda bi, hi, qi: (bi, 0, 2 * heads + hi)),
                  pl.BlockSpec((1, tq, hw), lambda bi, hi, qi: (bi, qi, 3 * heads + hi)),
                  pl.BlockSpec((t, HEAD_DIM), full),
                  pl.BlockSpec((t, HEAD_DIM), full),
                  pl.BlockSpec((4, HEAD_DIM), full),
                  pl.BlockSpec((1, hw), full)],
        out_specs=pl.BlockSpec((1, tq, hw), lambda bi, hi, qi: (bi, qi, hi)),
        out_shape=jax.ShapeDtypeStruct((b, t, heads * hw), BF16),
        scratch_shapes=[pltpu.VMEM((t, hw), BF16)],
        compiler_params=_params(("parallel", "parallel", "arbitrary")),
        name="diff_attn",
    )(proj, proj, proj, proj, cos, sin, lam, subln_w.reshape(1, hw))


def _na_kernel(q_ref, k_ref, v_ref, g_ref, bias_ref, o_ref, *, ctx_len, rows, scale):
    wk = NA_WR * GRID_W
    kc = k_ref[0, 0:ctx_len, :]
    vc = v_ref[0, 0:ctx_len, :]

    s = _dot_nt(q_ref[0, 0:ctx_len, :], kc) * scale
    m = jnp.max(s, axis=-1, keepdims=True)
    pr = jnp.exp(s - m)
    o = _dot(pr.astype(BF16), vc) / jnp.sum(pr, axis=-1, keepdims=True)
    o_ref[0, 0:ctx_len, :] = (o * _silu(g_ref[0, 0:ctx_len, :].astype(F32))).astype(BF16)

    def body(r, carry):
        rs = jnp.clip(r - NA_WR // 2, 0, rows - NA_WR)
        qs = pl.multiple_of(ctx_len + r * GRID_W, GRID_W)
        ks = pl.multiple_of(ctx_len + rs * GRID_W, GRID_W)
        q = q_ref[0, pl.ds(qs, GRID_W), :]
        s_win = _dot_nt(q, k_ref[0, pl.ds(ks, wk), :]) * scale + bias_ref[0, rs - r + NA_WR - 1]
        s_ctx = _dot_nt(q, kc) * scale
        mx = jnp.maximum(jnp.max(s_win, axis=-1, keepdims=True), jnp.max(s_ctx, axis=-1, keepdims=True))
        p_win = jnp.exp(s_win - mx)
        p_ctx = jnp.exp(s_ctx - mx)
        den = jnp.sum(p_win, axis=-1, keepdims=True) + jnp.sum(p_ctx, axis=-1, keepdims=True)
        out = (_dot(p_win.astype(BF16), v_ref[0, pl.ds(ks, wk), :]) + _dot(p_ctx.astype(BF16), vc)) / den
        gate = g_ref[0, pl.ds(qs, GRID_W), :].astype(F32)
        o_ref[0, pl.ds(qs, GRID_W), :] = (out * _silu(gate)).astype(BF16)
        return carry

    lax.fori_loop(0, rows, body, 0)


def _na_bias_table(rpb):
    heads = rpb.shape[0]
    cols = jnp.arange(GRID_W)
    start = jnp.clip(cols - NA_WC // 2, 0, GRID_W - NA_WC)
    inside = (cols[None, :] >= start[:, None]) & (cols[None, :] < start[:, None] + NA_WC)
    dc = jnp.clip(cols[None, :] - cols[:, None] + NA_WC - 1, 0, 2 * NA_WC - 2)
    dr = jnp.arange(NA_WR)[:, None] + jnp.arange(NA_WR)[None, :]
    tab = rpb[:, dr[:, :, None, None], dc[None, None, :, :]]
    tab = jnp.where(inside[None, None, None], tab.astype(F32), NEG_BIG)
    return tab.transpose(0, 1, 3, 2, 4).reshape(heads, NA_WR, GRID_W, NA_WR * GRID_W)


def _neighbourhood(proj, rpb, ctx_len, d_model):
    b, t, _ = proj.shape
    heads = d_model // HEAD_DIM
    rows = (t - ctx_len) // GRID_W
    assert rows >= NA_WR
    wk = NA_WR * GRID_W
    bias = _na_bias_table(rpb)
    col = lambda off: (lambda bi, hi: (bi, 0, off * heads + hi))
    return pl.pallas_call(
        functools.partial(_na_kernel, ctx_len=ctx_len, rows=rows, scale=HEAD_DIM ** -0.5),
        grid=(b, heads),
        in_specs=[pl.BlockSpec((1, t, HEAD_DIM), col(0)),
                  pl.BlockSpec((1, t, HEAD_DIM), col(1)),
                  pl.BlockSpec((1, t, HEAD_DIM), col(2)),
                  pl.BlockSpec((1, t, HEAD_DIM), col(3)),
                  pl.BlockSpec((1, NA_WR, GRID_W, wk), lambda bi, hi: (hi, 0, 0, 0))],
        out_specs=pl.BlockSpec((1, t, HEAD_DIM), col(0)),
        out_shape=jax.ShapeDtypeStruct((b, t, heads * HEAD_DIM), BF16),
        compiler_params=_params(("parallel", "parallel")),
        name="neighbourhood",
    )(proj, proj, proj, proj, bias)


def kernel(x, c, ctx, c_ctx, norm_w, ada_w, ada_b, dn_w_in, dn_conv_w, dn_a_log, dn_dt_bias, dn_onorm_w,
           dn_w_out, da_w_in, da_lambda, da_subln_w, da_w_out, na_w_in, na_rpb, na_w_out, final_norm_w):
    b, n, d = x.shape
    ctx_len = ctx.shape[1]
    t = ctx_len + n
    depth = norm_w.shape[0]
    assert b + 1 <= MOD_ROWS and n % GRID_W == 0 and ctx_len % DN_CHUNK == 0
    tm = t // 4
    assert t % 4 == 0 and tm % 16 == 0

    xa = jnp.concatenate([ctx, x], axis=1)
    cvec = jnp.concatenate([c, c_ctx[None, :], jnp.zeros((MOD_ROWS - b - 1, d), F32)], axis=0)
    mods = _modulation(cvec, ada_w, ada_b)

    conv_ch = 4 * d
    for i in range(depth):
        kind, j = i % N_MIXERS, i // N_MIXERS
        sh, sc, gt = mods[i, :, 0:d], mods[i, :, d:2 * d], mods[i, :, 2 * d:3 * d]
        lat = lambda m: m[0:b, None, :]
        cx = lambda m: m[b:b + 1, :]
        if kind == 0:
            w_in = dn_w_in[j]
            proj, ba = _norm_proj(xa, norm_w[i], lat(sc), lat(sh), cx(sc), cx(sh),
                                  w_in[:, 0:conv_ch + 2 * d].astype(BF16), w_in[:, conv_ch + 2 * d:].astype(BF16),
                                  ctx_len, tm)
            y = _deltanet(proj, ba, dn_conv_w[j], dn_a_log[j], dn_dt_bias[j], dn_onorm_w[j], ctx_len, d)
            w_out = dn_w_out[j]
        elif kind == 1:
            proj, _ = _norm_proj(xa, norm_w[i], lat(sc), lat(sh), cx(sc), cx(sh),
                                 da_w_in[j].astype(BF16), None, ctx_len, tm)
            y = _diff_attn(proj, da_lambda[j], da_subln_w[j], ctx_len, d, i)
            w_out = da_w_out[j]
        else:
            proj, _ = _norm_proj(xa, norm_w[i], lat(sc), lat(sh), cx(sc), cx(sh),
                                 na_w_in[j].astype(BF16), None, ctx_len, tm)
            y = _neighbourhood(proj, na_rpb[j], ctx_len, d)
            w_out = na_w_out[j]
        xa = _out_proj(y, w_out.astype(BF16), xa, lat(gt), cx(gt), ctx_len, tm)
    return _final_norm(xa, final_norm_w, ctx_len)
```

```python
import functools
import math

import jax
import jax.numpy as jnp
from jax import lax
from jax.experimental import pallas as pl
from jax.experimental.pallas import tpu as pltpu

F32 = jnp.float32
BF16 = jnp.bfloat16

GRID_W = 64
N_MIXERS = 3
RMS_EPS = 1e-6
HEAD_DIM = 128
DN_CONV_K = 5
DN_CHUNK = 64
DN_HEADS_PER_STEP = 16
DA_SUBLN_EPS = 1e-5
ROPE_THETA = 10000.0
NA_WR = 8
NA_WC = 16
NA_ROWS_PER_ITER = 8
NEG_BIG = -1e30
MOD_ROWS = 8
V7X_VMEM_LIMIT = 56 * 1024 * 1024


def _sigmoid(x):
    return 1.0 / (1.0 + jnp.exp(-x))


def _silu(x):
    return x * _sigmoid(x)


def _dot(a, b):
    return jnp.dot(a, b, preferred_element_type=F32)


def _dot_nt(a, b):
    return lax.dot_general(a, b, (((1,), (1,)), ((), ())), preferred_element_type=F32)


def _dot_tn(a, b):
    return lax.dot_general(a, b, (((0,), (0,)), ((), ())), preferred_element_type=F32)


def _params(semantics, vmem=V7X_VMEM_LIMIT):
    return pltpu.CompilerParams(dimension_semantics=semantics, vmem_limit_bytes=vmem)


def _mod_kernel(c_ref, w_ref, b_ref, o_ref):
    s = _silu(c_ref[...])
    o_ref[0] = jnp.dot(s, w_ref[0], preferred_element_type=F32,
                       precision=lax.Precision.HIGHEST) + b_ref[0]


def _modulation(cvec, ada_w, ada_b):
    depth, d, p = ada_w.shape
    tn = 1024 if p % 1024 == 0 else p
    return pl.pallas_call(
        _mod_kernel,
        grid=(depth, p // tn),
        in_specs=[pl.BlockSpec((MOD_ROWS, d), lambda i, n: (0, 0)),
                  pl.BlockSpec((1, d, tn), lambda i, n: (i, 0, n)),
                  pl.BlockSpec((1, 1, tn), lambda i, n: (i, 0, n))],
        out_specs=pl.BlockSpec((1, MOD_ROWS, tn), lambda i, n: (i, 0, n)),
        out_shape=jax.ShapeDtypeStruct((depth, MOD_ROWS, p), F32),
        compiler_params=_params(("parallel", "parallel")),
        name="adaln_mod",
    )(cvec, ada_w, ada_b.reshape(depth, 1, p))


def _norm_proj_kernel(*refs, ctx_len, tm, has_extra):
    if has_extra:
        x_ref, nw_ref, scl_ref, shl_ref, scc_ref, shc_ref, w_ref, wx_ref, o_ref, ox_ref, h_ref = refs
    else:
        x_ref, nw_ref, scl_ref, shl_ref, scc_ref, shc_ref, w_ref, o_ref, h_ref = refs
    mi = pl.program_id(1)
    ni = pl.program_id(2)

    @pl.when(ni == 0)
    def _():
        x = x_ref[0]
        ms = jnp.mean(x * x, axis=-1, keepdims=True)
        y = x * lax.rsqrt(ms + RMS_EPS) * nw_ref[...]
        row = lax.broadcasted_iota(jnp.int32, (tm, 1), 0) + mi * tm
        is_ctx = row < ctx_len
        sc = jnp.where(is_ctx, scc_ref[...], scl_ref[0])
        sh = jnp.where(is_ctx, shc_ref[...], shl_ref[0])
        h = (y * (1.0 + sc) + sh).astype(BF16)
        h_ref[...] = h
        if has_extra:
            ox_ref[0] = _dot(h, wx_ref[...])

    o_ref[0] = _dot(h_ref[...], w_ref[...]).astype(BF16)


def _norm_proj(xa, nw, scl, shl, scc, shc, w, wx, ctx_len, tm):
    b, t, d = xa.shape
    p = w.shape[1]
    tn = 512
    assert t % tm == 0 and p % tn == 0
    has_extra = wx is not None
    vec_l = pl.BlockSpec((1, 1, d), lambda bi, mi, ni: (bi, 0, 0))
    vec_c = pl.BlockSpec((1, d), lambda bi, mi, ni: (0, 0))
    in_specs = [pl.BlockSpec((1, tm, d), lambda bi, mi, ni: (bi, mi, 0)),
                vec_c, vec_l, vec_l, vec_c, vec_c,
                pl.BlockSpec((d, tn), lambda bi, mi, ni: (0, ni))]
    out_specs = [pl.BlockSpec((1, tm, tn), lambda bi, mi, ni: (bi, mi, ni))]
    out_shape = [jax.ShapeDtypeStruct((b, t, p), BF16)]
    args = [xa, nw.reshape(1, d), scl, shl, scc, shc, w]
    if has_extra:
        px = wx.shape[1]
        in_specs.append(pl.BlockSpec((d, px), lambda bi, mi, ni: (0, 0)))
        out_specs.append(pl.BlockSpec((1, tm, px), lambda bi, mi, ni: (bi, mi, 0)))
        out_shape.append(jax.ShapeDtypeStruct((b, t, px), F32))
        args.append(wx)
    outs = pl.pallas_call(
        functools.partial(_norm_proj_kernel, ctx_len=ctx_len, tm=tm, has_extra=has_extra),
        grid=(b, t // tm, p // tn),
        in_specs=in_specs,
        out_specs=out_specs,
        out_shape=out_shape,
        scratch_shapes=[pltpu.VMEM((tm, d), BF16)],
        compiler_params=_params(("parallel", "parallel", "arbitrary")),
        name="norm_proj",
    )(*args)
    return (outs[0], outs[1]) if has_extra else (outs[0], None)


def _out_proj_kernel(y_ref, w_ref, x_ref, gtl_ref, gtc_ref, o_ref, *, ctx_len, tm):
    mi = pl.program_id(1)
    acc = _dot(y_ref[0], w_ref[...])
    row = lax.broadcasted_iota(jnp.int32, (tm, 1), 0) + mi * tm
    gt = jnp.where(row < ctx_len, gtc_ref[...], gtl_ref[0])
    o_ref[0] = x_ref[0] + gt * acc


def _out_proj(y, w, xa, gtl, gtc, ctx_len, tm):
    b, t, d = xa.shape
    kd = y.shape[2]
    tn = 512
    assert d % tn == 0 and t % tm == 0
    return pl.pallas_call(
        functools.partial(_out_proj_kernel, ctx_len=ctx_len, tm=tm),
        grid=(b, t // tm, d // tn),
        in_specs=[pl.BlockSpec((1, tm, kd), lambda bi, mi, ni: (bi, mi, 0)),
                  pl.BlockSpec((kd, tn), lambda bi, mi, ni: (0, ni)),
                  pl.BlockSpec((1, tm, tn), lambda bi, mi, ni: (bi, mi, ni)),
                  pl.BlockSpec((1, 1, tn), lambda bi, mi, ni: (bi, 0, ni)),
                  pl.BlockSpec((1, tn), lambda bi, mi, ni: (0, ni))],
        out_specs=pl.BlockSpec((1, tm, tn), lambda bi, mi, ni: (bi, mi, ni)),
        out_shape=jax.ShapeDtypeStruct((b, t, d), F32),
        compiler_params=_params(("parallel", "parallel", "arbitrary")),
        name="out_proj",
    )(y, w, xa, gtl, gtc)


def _final_norm_kernel(x_ref, w_ref, o_ref):
    x = x_ref[0]
    ms = jnp.mean(x * x, axis=-1, keepdims=True)
    o_ref[0] = x * lax.rsqrt(ms + RMS_EPS) * w_ref[...]


def _final_norm(xa, w, ctx_len):
    b, t, d = xa.shape
    n = t - ctx_len
    tr = math.gcd(ctx_len, 256)
    assert n % tr == 0
    off = ctx_len // tr
    return pl.pallas_call(
        _final_norm_kernel,
        grid=(b, n // tr),
        in_specs=[pl.BlockSpec((1, tr, d), lambda bi, ri: (bi, ri + off, 0)),
                  pl.BlockSpec((1, d), lambda bi, ri: (0, 0))],
        out_specs=pl.BlockSpec((1, tr, d), lambda bi, ri: (bi, ri, 0)),
        out_shape=jax.ShapeDtypeStruct((b, n, d), F32),
        compiler_params=_params(("parallel", "parallel")),
        name="final_norm",
    )(xa, w.reshape(1, d))


def _dn_conv_kernel(x_ref, w_ref, o_ref, *, ctx_len, n_qk_blocks):
    blk = pl.program_id(1)
    x = x_ref[0].astype(F32)
    t = x.shape[0]
    w = w_ref[...]
    row = lax.broadcasted_iota(jnp.int32, x.shape, 0)
    seg = row >= ctx_len
    half = DN_CONV_K // 2
    acc = x * w[half:half + 1, :]
    for tap in range(DN_CONV_K):
        dlt = tap - half
        if dlt == 0:
            continue
        shifted = pltpu.roll(x, (-dlt) % t, 0)
        src = row + dlt
        ok = (src >= 0) & (src < t) & ((src >= ctx_len) == seg)
        acc = acc + jnp.where(ok, shifted, 0.0) * w[tap:tap + 1, :]
    y = _silu(acc)
    inv = lax.rsqrt(jnp.sum(y * y, axis=-1, keepdims=True) + 1e-6)
    y = y * jnp.where(blk < n_qk_blocks, inv, 1.0)
    o_ref[0] = y.astype(BF16)


def _dn_conv(proj, conv_w_t, ctx_len, conv_ch, n_qk_blocks):
    b, t, _ = proj.shape
    return pl.pallas_call(
        functools.partial(_dn_conv_kernel, ctx_len=ctx_len, n_qk_blocks=n_qk_blocks),
        grid=(b, conv_ch // HEAD_DIM),
        in_specs=[pl.BlockSpec((1, t, HEAD_DIM), lambda bi, ci: (bi, 0, ci)),
                  pl.BlockSpec((DN_CONV_K, HEAD_DIM), lambda bi, ci: (0, ci))],
        out_specs=pl.BlockSpec((1, t, HEAD_DIM), lambda bi, ci: (bi, 0, ci)),
        out_shape=jax.ShapeDtypeStruct((b, t, conv_ch), BF16),
        compiler_params=_params(("parallel", "parallel")),
        name="dn_conv",
    )(proj, conv_w_t)


def _exact_f32_dot(tri, g):
    hi = g.astype(BF16)
    r1 = g - hi.astype(F32)
    mid = r1.astype(BF16)
    lo = (r1 - mid.astype(F32)).astype(BF16)
    return _dot(tri, hi) + _dot(tri, mid) + _dot(tri, lo)


def _dn_gate_kernel(ba_ref, alog_ref, dtb_ref, beta_ref, gc_ref, *, hv):
    ba = ba_ref[0]
    z = ba + dtb_ref[...]
    softplus = jnp.maximum(z, 0.0) + jnp.log1p(jnp.exp(-jnp.abs(z)))
    g = -jnp.exp(alog_ref[...]) * softplus
    beta = _sigmoid(ba)
    ii = lax.broadcasted_iota(jnp.int32, (DN_CHUNK, DN_CHUNK), 0)
    jj = lax.broadcasted_iota(jnp.int32, (DN_CHUNK, DN_CHUNK), 1)
    lower = jnp.where(ii >= jj, 1.0, 0.0).astype(BF16)
    upper = jnp.where(ii <= jj, 1.0, 0.0).astype(BF16)
    gc_fwd = _exact_f32_dot(lower, g)
    gc_bwd = _exact_f32_dot(upper, g)
    beta_ref[0, 0] = beta[:, 0:hv]
    beta_ref[0, 1] = beta[:, 2 * hv:3 * hv]
    gc_ref[0, 0] = gc_fwd[:, hv:2 * hv]
    gc_ref[0, 1] = gc_bwd[:, 3 * hv:4 * hv]


def _dn_gates(ba, alog_row, dtb_row, hv):
    b, t, pw = ba.shape
    out = jax.ShapeDtypeStruct((b, 2, t, hv), F32)
    ospec = pl.BlockSpec((1, 2, DN_CHUNK, hv), lambda bi, ci: (bi, 0, ci, 0))
    return pl.pallas_call(
        functools.partial(_dn_gate_kernel, hv=hv),
        grid=(b, t // DN_CHUNK),
        in_specs=[pl.BlockSpec((1, DN_CHUNK, pw), lambda bi, ci: (bi, ci, 0)),
                  pl.BlockSpec((1, pw), lambda bi, ci: (0, 0)),
                  pl.BlockSpec((1, pw), lambda bi, ci: (0, 0))],
        out_specs=[ospec, ospec],
        out_shape=[out, out],
        compiler_params=_params(("parallel", "parallel")),
        name="dn_gates",
    )(ba, alog_row, dtb_row)


def _dn_chunk_kernel(q_ref, k_ref, v_ref, beta_ref, gcc_ref, gcr_ref, o_ref, s_ref, *, heads, scale):
    d = pl.program_id(2)
    p = pl.program_id(3)

    @pl.when(p == 0)
    def _():
        s_ref[...] = jnp.zeros_like(s_ref)

    c = DN_CHUNK
    ii = lax.broadcasted_iota(jnp.int32, (c, c), 0)
    jj = lax.broadcasted_iota(jnp.int32, (c, c), 1)
    order = (ii - jj) * (1 - 2 * d)
    incl = order >= 0
    strict = order > 0
    eye = jnp.where(ii == jj, 1.0, 0.0)
    pair_masks = []
    sz = 1
    while sz < c:
        pair_masks.append((ii // (2 * sz) == jj // (2 * sz)) & (ii // sz != jj // sz))
        sz *= 2
    fwd = d == 0
    hr = range(heads)
    lanes = lambda i: slice(i * HEAD_DIM, (i + 1) * HEAD_DIM)

    kf, qf, kk, qk = [], [], [], []
    for m in range(heads // 2):
        q = q_ref[0, :, lanes(m)]
        k = k_ref[0, :, lanes(m)]
        both = _dot_nt(jnp.concatenate([k, q], axis=0), k)
        kk.append(both[0:c])
        qk.append(both[c:2 * c] * scale)
        kf.append(k.astype(F32))
        qf.append(q.astype(F32))
    beta = [beta_ref[0, 0, 0, :, h:h + 1] for h in hr]
    gcol = [gcc_ref[0, 0, 0, :, h:h + 1] for h in hr]
    grow = [gcr_ref[0, 0, 0, 0, h:h + 1, :] for h in hr]
    gam = [jnp.where(incl, jnp.exp(jnp.where(incl, gcol[h] - grow[h], 0.0)), 0.0) for h in hr]
    low = [jnp.where(strict, beta[h] * kk[h // 2] * gam[h], 0.0) for h in hr]
    tinv = [eye - jnp.where(pair_masks[0], low[h], 0.0) for h in hr]
    for pm in pair_masks[1:]:
        tb = [tinv[h].astype(BF16) for h in hr]
        xc = [_dot(tb[h], jnp.where(pm, low[h], 0.0).astype(BF16)) for h in hr]
        tinv = [tinv[h] - _dot(xc[h].astype(BF16), tb[h]) for h in hr]
    tb = [tinv[h].astype(BF16) for h in hr]
    eg = [jnp.exp(gcol[h]) for h in hr]
    glast = [jnp.where(fwd, gcol[h][c - 1:c, :], gcol[h][0:1, :]) for h in hr]
    s_old = [s_ref[h] for h in hr]
    sb = [s_old[h].astype(BF16) for h in hr]
    ks_qs = [_dot(jnp.concatenate([(eg[h] * kf[h // 2]).astype(BF16),
                                   (qf[h // 2] * (scale * eg[h])).astype(BF16)], axis=0), sb[h]) for h in hr]
    rhs = [(beta[h] * (v_ref[0, :, lanes(h)].astype(F32) - ks_qs[h][0:c])).astype(BF16) for h in hr]
    vnb = [_dot(tb[h], rhs[h]).astype(BF16) for h in hr]
    for h in hr:
        o = ks_qs[h][c:2 * c] + _dot((qk[h // 2] * gam[h]).astype(BF16), vnb[h])
        o_ref[0, 0, :, lanes(h)] = o.astype(BF16)
    for h in hr:
        k_dec = (kf[h // 2] * jnp.exp(glast[h] - gcol[h])).astype(BF16)
        s_ref[h] = s_old[h] * jnp.exp(glast[h]) + _dot_tn(k_dec, vnb[h])


def _dn_chunks(qkv, beta_g, gcc_g, gcr_g, ctx_len, qk_w, hv):
    b, t, _ = qkv.shape
    g = beta_g.shape[-1]
    assert hv % g == 0 and qk_w % (g // 2 * HEAD_DIM) == 0
    nc = t // DN_CHUNK
    nctx = ctx_len // DN_CHUNK
    qw = g // 2 * HEAD_DIM
    vw = g * HEAD_DIM
    k_off = qk_w // qw
    v_off = 2 * qk_w // vw

    def chunk(d, p):
        back = jnp.where(p < nctx, nctx - 1 - p, nc + nctx - 1 - p)
        return jnp.where(d == 0, p, back)

    return pl.pallas_call(
        functools.partial(_dn_chunk_kernel, heads=g, scale=HEAD_DIM ** -0.5),
        grid=(b, hv // g, 2, nc),
        in_specs=[pl.BlockSpec((1, DN_CHUNK, qw), lambda bi, gi, d, p: (bi, chunk(d, p), gi)),
                  pl.BlockSpec((1, DN_CHUNK, qw), lambda bi, gi, d, p: (bi, chunk(d, p), k_off + gi)),
                  pl.BlockSpec((1, DN_CHUNK, vw), lambda bi, gi, d, p: (bi, chunk(d, p), v_off + gi)),
                  pl.BlockSpec((1, 1, 1, DN_CHUNK, g), lambda bi, gi, d, p: (bi, d, gi, chunk(d, p), 0)),
                  pl.BlockSpec((1, 1, 1, DN_CHUNK, g), lambda bi, gi, d, p: (bi, d, gi, chunk(d, p), 0)),
                  pl.BlockSpec((1, 1, 1, 1, g, DN_CHUNK), lambda bi, gi, d, p: (bi, d, gi, chunk(d, p), 0, 0))],
        out_specs=pl.BlockSpec((1, 1, DN_CHUNK, vw), lambda bi, gi, d, p: (bi, d, chunk(d, p), gi)),
        out_shape=jax.ShapeDtypeStruct((b, 2, t, hv * HEAD_DIM), BF16),
        scratch_shapes=[pltpu.VMEM((g, HEAD_DIM, HEAD_DIM), F32)],
        compiler_params=_params(("parallel", "parallel", "arbitrary", "arbitrary")),
        name="dn_chunks",
    )(qkv, qkv, qkv, beta_g, gcc_g, gcr_g)


def _dn_finish_kernel(o_ref, z_ref, w_ref, y_ref):
    cw = y_ref.shape[2]
    for h in range(cw // HEAD_DIM):
        hl = slice(h * HEAD_DIM, (h + 1) * HEAD_DIM)
        o = o_ref[0, 0, :, hl].astype(F32) + o_ref[0, 1, :, hl].astype(F32)
        ms = jnp.mean(o * o, axis=-1, keepdims=True)
        y = o * lax.rsqrt(ms + RMS_EPS) * w_ref[...]
        y_ref[0, :, hl] = (y * _silu(z_ref[0, :, hl].astype(F32))).astype(BF16)


def _dn_finish(o, proj, onorm_w, z_off):
    b, _, t, vw = o.shape
    cw = 1024 if vw % 1024 == 0 else vw
    assert z_off % cw == 0
    tr = math.gcd(t, 256)
    return pl.pallas_call(
        _dn_finish_kernel,
        grid=(b, t // tr, vw // cw),
        in_specs=[pl.BlockSpec((1, 2, tr, cw), lambda bi, ri, ci: (bi, 0, ri, ci)),
                  pl.BlockSpec((1, tr, cw), lambda bi, ri, ci: (bi, ri, z_off // cw + ci)),
                  pl.BlockSpec((1, HEAD_DIM), lambda bi, ri, ci: (0, 0))],
        out_specs=pl.BlockSpec((1, tr, cw), lambda bi, ri, ci: (bi, ri, ci)),
        out_shape=jax.ShapeDtypeStruct((b, t, vw), BF16),
        compiler_params=_params(("parallel", "parallel", "parallel")),
        name="dn_finish",
    )(o, proj, onorm_w.reshape(1, HEAD_DIM))


def _deltanet(proj, ba, conv_w, a_log, dt_bias, onorm_w, ctx_len, d_model):
    hk = d_model // HEAD_DIM
    hv = 2 * hk
    qk_w = hk * HEAD_DIM
    conv_ch = 2 * qk_w + hv * HEAD_DIM
    b, t, _ = proj.shape
    g = min(DN_HEADS_PER_STEP, hv)
    qkv = _dn_conv(proj, jnp.transpose(conv_w), ctx_len, conv_ch, 2 * hk)
    zeros = jnp.zeros((hv,), F32)
    alog_row = jnp.concatenate([zeros, a_log[0], zeros, a_log[1]]).reshape(1, 4 * hv)
    dtb_row = jnp.concatenate([zeros, dt_bias[0], zeros, dt_bias[1]]).reshape(1, 4 * hv)
    beta, gc = _dn_gates(ba, alog_row, dtb_row, hv)
    nc = t // DN_CHUNK
    beta_g = beta.reshape(b, 2, t, hv // g, g).transpose(0, 1, 3, 2, 4)
    gcc_g = gc.reshape(b, 2, t, hv // g, g).transpose(0, 1, 3, 2, 4)
    gcr_g = gc.reshape(b, 2, nc, DN_CHUNK, hv // g, g).transpose(0, 1, 4, 2, 5, 3)
    o = _dn_chunks(qkv, beta_g, gcc_g, gcr_g, ctx_len, qk_w, hv)
    return _dn_finish(o, proj, onorm_w, conv_ch)


def _rope(x, cos, sin):
    quarter = HEAD_DIM // 4
    lane = lax.broadcasted_iota(jnp.int32, x.shape, 1)
    up = pltpu.roll(x, HEAD_DIM - quarter, 1)
    down = pltpu.roll(x, quarter, 1)
    rot = jnp.where((lane % (2 * quarter)) < quarter, -up, down)
    return x * cos + rot * sin


def _da_kernel(q_ref, k_ref, v_ref, g_ref, cos_ref, sin_ref, lam_ref, sw_ref, o_ref, krot_ref,
               *, ctx_len, lambda_init, scale):
    qi = pl.program_id(2)
    t = k_ref.shape[1]
    tq = q_ref.shape[1]

    @pl.when(qi == 0)
    def _():
        for j in range(2):
            lanes = slice(j * HEAD_DIM, (j + 1) * HEAD_DIM)
            kk = k_ref[0, :, lanes].astype(F32)
            krot_ref[:, lanes] = _rope(kk, cos_ref[...], sin_ref[...]).astype(BF16)

    lam = lam_ref[...]
    lam_full = (jnp.exp(jnp.sum(lam[0:1] * lam[1:2], axis=-1, keepdims=True))
                - jnp.exp(jnp.sum(lam[2:3] * lam[3:4], axis=-1, keepdims=True)) + lambda_init)
    qs = pl.multiple_of(qi * tq, tq)
    cos_q = cos_ref[pl.ds(qs, tq), :]
    sin_q = sin_ref[pl.ds(qs, tq), :]

    def attend(nk):
        outs = []
        for j in range(2):
            lanes = slice(j * HEAD_DIM, (j + 1) * HEAD_DIM)
            q = _rope(q_ref[0, :, lanes].astype(F32), cos_q, sin_q).astype(BF16)
            s = _dot_nt(q, krot_ref[0:nk, lanes]) * scale
            m = jnp.max(s, axis=-1, keepdims=True)
            pr = jnp.exp(s - m)
            l = jnp.sum(pr, axis=-1, keepdims=True)
            outs.append(_dot(pr.astype(BF16), v_ref[0, 0:nk, :]) / l)
        o = outs[0] - lam_full * outs[1]
        ms = jnp.mean(o * o, axis=-1, keepdims=True)
        y = o * lax.rsqrt(ms + DA_SUBLN_EPS) * sw_ref[...] * (1.0 - lambda_init)
        o_ref[0] = (y * _silu(g_ref[0].astype(F32))).astype(BF16)

    @pl.when(qi == 0)
    def _():
        attend(ctx_len)

    @pl.when(qi > 0)
    def _():
        attend(t)


def _rope_tables(ctx_len, n):
    tok = jnp.arange(n)
    row = (tok // GRID_W).astype(F32)
    col = (tok % GRID_W).astype(F32)
    half = HEAD_DIM // 2
    inv = ROPE_THETA ** (-jnp.arange(0, half, 2, dtype=F32) / half)
    ang_r = row[:, None] * inv
    ang_c = col[:, None] * inv
    ang = jnp.concatenate([ang_r, ang_r, ang_c, ang_c], axis=-1)
    cos = jnp.concatenate([jnp.ones((ctx_len, HEAD_DIM), F32), jnp.cos(ang)], axis=0)
    sin = jnp.concatenate([jnp.zeros((ctx_len, HEAD_DIM), F32), jnp.sin(ang)], axis=0)
    return cos, sin


def _diff_attn(proj, lam, subln_w, ctx_len, d_model, layer_idx):
    b, t, _ = proj.shape
    hw = 2 * HEAD_DIM
    heads = d_model // hw
    tq = ctx_len
    assert t % tq == 0
    lambda_init = 0.8 - 0.6 * math.exp(-0.3 * layer_idx)
    cos, sin = _rope_tables(ctx_len, t - ctx_len)
    full = lambda bi, hi, qi: (0, 0)
    return pl.pallas_call(
        functools.partial(_da_kernel, ctx_len=ctx_len, lambda_init=lambda_init, scale=HEAD_DIM ** -0.5),
        grid=(b, heads, t // tq),
        in_specs=[pl.BlockSpec((1, tq, hw), lambda bi, hi, qi: (bi, qi, hi)),
                  pl.BlockSpec((1, t, hw), lambda bi, hi, qi: (bi, 0, heads + hi)),
                  pl.BlockSpec((1, t, hw), lambda bi, hi, qi: (bi, 0, 2 * heads + hi)),
                  pl.BlockSpec((1, tq, hw), lambda bi, hi, qi: (bi, qi, 3 * heads + hi)),
                  pl.BlockSpec((t, HEAD_DIM), full),
                  pl.BlockSpec((t, HEAD_DIM), full),
                  pl.BlockSpec((4, HEAD_DIM), full),
                  pl.BlockSpec((1, hw), full)],
        out_specs=pl.BlockSpec((1, tq, hw), lambda bi, hi, qi: (bi, qi, hi)),
        out_shape=jax.ShapeDtypeStruct((b, t, heads * hw), BF16),
        scratch_shapes=[pltpu.VMEM((t, hw), BF16)],
        compiler_params=_params(("parallel", "parallel", "arbitrary")),
        name="diff_attn",
    )(proj, proj, proj, proj, cos, sin, lam, subln_w.reshape(1, hw))


def _na_kernel(q_ref, k_ref, v_ref, g_ref, bias_ref, o_ref, *, ctx_len, rows, scale):
    wk = NA_WR * GRID_W
    kc = k_ref[0, 0:ctx_len, :]
    vc = v_ref[0, 0:ctx_len, :]

    s = _dot_nt(q_ref[0, 0:ctx_len, :], kc) * scale
    m = jnp.max(s, axis=-1, keepdims=True)
    pr = jnp.exp(s - m)
    o = _dot(pr.astype(BF16), vc) / jnp.sum(pr, axis=-1, keepdims=True)
    o_ref[0, 0:ctx_len, :] = (o * _silu(g_ref[0, 0:ctx_len, :].astype(F32))).astype(BF16)

    def body(it, carry):
        nr = range(NA_ROWS_PER_ITER)
        r = [it * NA_ROWS_PER_ITER + i for i in nr]
        rs = [jnp.clip(r[i] - NA_WR // 2, 0, rows - NA_WR) for i in nr]
        qs = [pl.multiple_of(ctx_len + r[i] * GRID_W, GRID_W) for i in nr]
        ks = [pl.multiple_of(ctx_len + rs[i] * GRID_W, GRID_W) for i in nr]
        q = [q_ref[0, pl.ds(qs[i], GRID_W), :] for i in nr]
        s_win = [_dot_nt(q[i], k_ref[0, pl.ds(ks[i], wk), :]) * scale + bias_ref[0, rs[i] - r[i] + NA_WR - 1]
                 for i in nr]
        s_ctx = [_dot_nt(q[i], kc) * scale for i in nr]
        mx = [jnp.maximum(jnp.max(s_win[i], axis=-1, keepdims=True), jnp.max(s_ctx[i], axis=-1, keepdims=True))
              for i in nr]
        p_win = [jnp.exp(s_win[i] - mx[i]) for i in nr]
        p_ctx = [jnp.exp(s_ctx[i] - mx[i]) for i in nr]
        den = [jnp.sum(p_win[i], axis=-1, keepdims=True) + jnp.sum(p_ctx[i], axis=-1, keepdims=True) for i in nr]
        out = [(_dot(p_win[i].astype(BF16), v_ref[0, pl.ds(ks[i], wk), :]) + _dot(p_ctx[i].astype(BF16), vc)) / den[i]
               for i in nr]
        for i in nr:
            gate = g_ref[0, pl.ds(qs[i], GRID_W), :].astype(F32)
            o_ref[0, pl.ds(qs[i], GRID_W), :] = (out[i] * _silu(gate)).astype(BF16)
        return carry

    lax.fori_loop(0, rows // NA_ROWS_PER_ITER, body, 0)


def _na_bias_table(rpb):
    heads = rpb.shape[0]
    cols = jnp.arange(GRID_W)
    start = jnp.clip(cols - NA_WC // 2, 0, GRID_W - NA_WC)
    inside = (cols[None, :] >= start[:, None]) & (cols[None, :] < start[:, None] + NA_WC)
    dc = cols[None, :] - cols[:, None] + NA_WC - 1
    rpb = rpb.astype(F32)
    wide = jnp.full((heads, 2 * NA_WR - 1, GRID_W, GRID_W), NEG_BIG, F32)
    for off in range(2 * NA_WC - 1):
        wide = jnp.where((inside & (dc == off))[None, None], rpb[:, :, off, None, None], wide)
    tab = jnp.stack([wide[:, o:o + NA_WR] for o in range(NA_WR)], axis=1)
    return tab.transpose(0, 1, 3, 2, 4).reshape(heads, NA_WR, GRID_W, NA_WR * GRID_W)


def _neighbourhood(proj, rpb, ctx_len, d_model):
    b, t, _ = proj.shape
    heads = d_model // HEAD_DIM
    rows = (t - ctx_len) // GRID_W
    assert rows >= NA_WR
    wk = NA_WR * GRID_W
    bias = _na_bias_table(rpb)
    col = lambda off: (lambda bi, hi: (bi, 0, off * heads + hi))
    return pl.pallas_call(
        functools.partial(_na_kernel, ctx_len=ctx_len, rows=rows, scale=HEAD_DIM ** -0.5),
        grid=(b, heads),
        in_specs=[pl.BlockSpec((1, t, HEAD_DIM), col(0)),
                  pl.BlockSpec((1, t, HEAD_DIM), col(1)),
                  pl.BlockSpec((1, t, HEAD_DIM), col(2)),
                  pl.BlockSpec((1, t, HEAD_DIM), col(3)),
                  pl.BlockSpec((1, NA_WR, GRID_W, wk), lambda bi, hi: (hi, 0, 0, 0))],
        out_specs=pl.BlockSpec((1, t, HEAD_DIM), col(0)),
        out_shape=jax.ShapeDtypeStruct((b, t, heads * HEAD_DIM), BF16),
        compiler_params=_params(("parallel", "parallel")),
        name="neighbourhood",
    )(proj, proj, proj, proj, bias)


def kernel(x, c, ctx, c_ctx, norm_w, ada_w, ada_b, dn_w_in, dn_conv_w, dn_a_log, dn_dt_bias, dn_onorm_w,
           dn_w_out, da_w_in, da_lambda, da_subln_w, da_w_out, na_w_in, na_rpb, na_w_out, final_norm_w):
    b, n, d = x.shape
    ctx_len = ctx.shape[1]
    t = ctx_len + n
    depth = norm_w.shape[0]
    assert b + 1 <= MOD_ROWS and n % GRID_W == 0 and ctx_len % DN_CHUNK == 0
    tm = t // 4
    assert t % 4 == 0 and tm % 16 == 0

    xa = jnp.concatenate([ctx, x], axis=1)
    cvec = jnp.concatenate([c, c_ctx[None, :], jnp.zeros((MOD_ROWS - b - 1, d), F32)], axis=0)
    mods = _modulation(cvec, ada_w, ada_b)

    conv_ch = 4 * d
    for i in range(depth):
        kind, j = i % N_MIXERS, i // N_MIXERS
        sh, sc, gt = mods[i, :, 0:d], mods[i, :, d:2 * d], mods[i, :, 2 * d:3 * d]
        lat = lambda m: m[0:b, None, :]
        cx = lambda m: m[b:b + 1, :]
        if kind == 0:
            w_in = dn_w_in[j]
            proj, ba = _norm_proj(xa, norm_w[i], lat(sc), lat(sh), cx(sc), cx(sh),
                                  w_in[:, 0:conv_ch + 2 * d].astype(BF16), w_in[:, conv_ch + 2 * d:].astype(BF16),
                                  ctx_len, tm)
            y = _deltanet(proj, ba, dn_conv_w[j], dn_a_log[j], dn_dt_bias[j], dn_onorm_w[j], ctx_len, d)
            w_out = dn_w_out[j]
        elif kind == 1:
            proj, _ = _norm_proj(xa, norm_w[i], lat(sc), lat(sh), cx(sc), cx(sh),
                                 da_w_in[j].astype(BF16), None, ctx_len, tm)
            y = _diff_attn(proj, da_lambda[j], da_subln_w[j], ctx_len, d, i)
            w_out = da_w_out[j]
        else:
            proj, _ = _norm_proj(xa, norm_w[i], lat(sc), lat(sh), cx(sc), cx(sh),
                                 na_w_in[j].astype(BF16), None, ctx_len, tm)
            y = _neighbourhood(proj, na_rpb[j], ctx_len, d)
            w_out = na_w_out[j]
        xa = _out_proj(y, w_out.astype(BF16), xa, lat(gt), cx(gt), ctx_len, tm)
    return _final_norm(xa, final_norm_w, ctx_len)
```

```python
import functools
import math

import jax
import jax.numpy as jnp
from jax import lax
from jax.experimental import pallas as pl
from jax.experimental.pallas import tpu as pltpu

F32 = jnp.float32
BF16 = jnp.bfloat16

GRID_W = 64
N_MIXERS = 3
RMS_EPS = 1e-6
HEAD_DIM = 128
DN_CONV_K = 5
DN_CHUNK = 64
DN_HEADS_PER_STEP = 32
DA_SUBLN_EPS = 1e-5
DA_KEY_CHUNK = 512
ROPE_THETA = 10000.0
NA_WR = 8
NA_WC = 16
NA_ROWS_PER_ITER = 8
NEG_BIG = -1e30
PROJ_TN = 512
MOD_ROWS = 8
V7X_VMEM_LIMIT = 56 * 1024 * 1024


def _sigmoid(x):
    return 1.0 / (1.0 + jnp.exp(-x))


def _silu(x):
    return x * _sigmoid(x)


def _dot(a, b):
    return jnp.dot(a, b, preferred_element_type=F32)


def _dot_nt(a, b):
    return lax.dot_general(a, b, (((1,), (1,)), ((), ())), preferred_element_type=F32)


def _dot_tn(a, b):
    return lax.dot_general(a, b, (((0,), (0,)), ((), ())), preferred_element_type=F32)


def _params(semantics, vmem=V7X_VMEM_LIMIT):
    return pltpu.CompilerParams(dimension_semantics=semantics, vmem_limit_bytes=vmem)


def _mod_kernel(c_ref, w_ref, b_ref, o_ref):
    s = _silu(c_ref[...])
    o_ref[0] = jnp.dot(s, w_ref[0], preferred_element_type=F32,
                       precision=lax.Precision.HIGHEST) + b_ref[0]


def _modulation(cvec, ada_w, ada_b):
    depth, d, p = ada_w.shape
    tn = 1024 if p % 1024 == 0 else p
    return pl.pallas_call(
        _mod_kernel,
        grid=(depth, p // tn),
        in_specs=[pl.BlockSpec((MOD_ROWS, d), lambda i, n: (0, 0)),
                  pl.BlockSpec((1, d, tn), lambda i, n: (i, 0, n)),
                  pl.BlockSpec((1, 1, tn), lambda i, n: (i, 0, n))],
        out_specs=pl.BlockSpec((1, MOD_ROWS, tn), lambda i, n: (i, 0, n)),
        out_shape=jax.ShapeDtypeStruct((depth, MOD_ROWS, p), F32),
        compiler_params=_params(("parallel", "parallel")),
        name="adaln_mod",
    )(cvec, ada_w, ada_b.reshape(depth, 1, p))


def _norm_proj_kernel(*refs, ctx_len, tm, has_extra):
    if has_extra:
        x_ref, nw_ref, scl_ref, shl_ref, scc_ref, shc_ref, w_ref, wx_ref, o_ref, ox_ref, h_ref = refs
    else:
        x_ref, nw_ref, scl_ref, shl_ref, scc_ref, shc_ref, w_ref, o_ref, h_ref = refs
    mi = pl.program_id(1)
    ni = pl.program_id(2)

    @pl.when(ni == 0)
    def _():
        x = x_ref[0]
        ms = jnp.mean(x * x, axis=-1, keepdims=True)
        y = x * lax.rsqrt(ms + RMS_EPS) * nw_ref[...]
        row = lax.broadcasted_iota(jnp.int32, (tm, 1), 0) + mi * tm
        is_ctx = row < ctx_len
        sc = jnp.where(is_ctx, scc_ref[...], scl_ref[0])
        sh = jnp.where(is_ctx, shc_ref[...], shl_ref[0])
        h = (y * (1.0 + sc) + sh).astype(BF16)
        h_ref[...] = h
        if has_extra:
            ox_ref[0] = _dot(h, wx_ref[...])

    o_ref[0] = _dot(h_ref[...], w_ref[...]).astype(BF16)


def _norm_proj(xa, nw, scl, shl, scc, shc, w, wx, ctx_len, tm):
    b, t, d = xa.shape
    p = w.shape[1]
    tn = PROJ_TN if p % PROJ_TN == 0 else PROJ_TN // 2
    assert t % tm == 0 and p % tn == 0
    has_extra = wx is not None
    vec_l = pl.BlockSpec((1, 1, d), lambda bi, mi, ni: (bi, 0, 0))
    vec_c = pl.BlockSpec((1, d), lambda bi, mi, ni: (0, 0))
    in_specs = [pl.BlockSpec((1, tm, d), lambda bi, mi, ni: (bi, mi, 0)),
                vec_c, vec_l, vec_l, vec_c, vec_c,
                pl.BlockSpec((d, tn), lambda bi, mi, ni: (0, ni))]
    out_specs = [pl.BlockSpec((1, tm, tn), lambda bi, mi, ni: (bi, mi, ni))]
    out_shape = [jax.ShapeDtypeStruct((b, t, p), BF16)]
    args = [xa, nw.reshape(1, d), scl, shl, scc, shc, w]
    if has_extra:
        px = wx.shape[1]
        in_specs.append(pl.BlockSpec((d, px), lambda bi, mi, ni: (0, 0)))
        out_specs.append(pl.BlockSpec((1, tm, px), lambda bi, mi, ni: (bi, mi, 0)))
        out_shape.append(jax.ShapeDtypeStruct((b, t, px), F32))
        args.append(wx)
    outs = pl.pallas_call(
        functools.partial(_norm_proj_kernel, ctx_len=ctx_len, tm=tm, has_extra=has_extra),
        grid=(b, t // tm, p // tn),
        in_specs=in_specs,
        out_specs=out_specs,
        out_shape=out_shape,
        scratch_shapes=[pltpu.VMEM((tm, d), BF16)],
        compiler_params=_params(("parallel", "parallel", "arbitrary")),
        name="norm_proj",
    )(*args)
    return (outs[0], outs[1]) if has_extra else (outs[0], None)


def _out_proj_kernel(y_ref, w_ref, x_ref, gtl_ref, gtc_ref, o_ref, *, ctx_len, tm):
    mi = pl.program_id(1)
    acc = _dot(y_ref[0], w_ref[...])
    row = lax.broadcasted_iota(jnp.int32, (tm, 1), 0) + mi * tm
    gt = jnp.where(row < ctx_len, gtc_ref[...], gtl_ref[0])
    o_ref[0] = x_ref[0] + gt * acc


def _out_proj(y, w, xa, gtl, gtc, ctx_len, tm):
    b, t, d = xa.shape
    kd = y.shape[2]
    tn = PROJ_TN if d % PROJ_TN == 0 else PROJ_TN // 2
    while 2 * (tm * kd * 2 + kd * tn * 2 + 2 * tm * tn * 4) > V7X_VMEM_LIMIT * 3 // 4:
        tn //= 2
    assert d % tn == 0 and t % tm == 0
    return pl.pallas_call(
        functools.partial(_out_proj_kernel, ctx_len=ctx_len, tm=tm),
        grid=(b, t // tm, d // tn),
        in_specs=[pl.BlockSpec((1, tm, kd), lambda bi, mi, ni: (bi, mi, 0)),
                  pl.BlockSpec((kd, tn), lambda bi, mi, ni: (0, ni)),
                  pl.BlockSpec((1, tm, tn), lambda bi, mi, ni: (bi, mi, ni)),
                  pl.BlockSpec((1, 1, tn), lambda bi, mi, ni: (bi, 0, ni)),
                  pl.BlockSpec((1, tn), lambda bi, mi, ni: (0, ni))],
        out_specs=pl.BlockSpec((1, tm, tn), lambda bi, mi, ni: (bi, mi, ni)),
        out_shape=jax.ShapeDtypeStruct((b, t, d), F32),
        compiler_params=_params(("parallel", "parallel", "arbitrary")),
        name="out_proj",
    )(y, w, xa, gtl, gtc)


def _final_norm_kernel(x_ref, w_ref, o_ref):
    x = x_ref[0]
    ms = jnp.mean(x * x, axis=-1, keepdims=True)
    o_ref[0] = x * lax.rsqrt(ms + RMS_EPS) * w_ref[...]


def _final_norm(xa, w, ctx_len):
    b, t, d = xa.shape
    n = t - ctx_len
    tr = math.gcd(ctx_len, 256)
    assert n % tr == 0
    off = ctx_len // tr
    return pl.pallas_call(
        _final_norm_kernel,
        grid=(b, n // tr),
        in_specs=[pl.BlockSpec((1, tr, d), lambda bi, ri: (bi, ri + off, 0)),
                  pl.BlockSpec((1, d), lambda bi, ri: (0, 0))],
        out_specs=pl.BlockSpec((1, tr, d), lambda bi, ri: (bi, ri, 0)),
        out_shape=jax.ShapeDtypeStruct((b, n, d), F32),
        compiler_params=_params(("parallel", "parallel")),
        name="final_norm",
    )(xa, w.reshape(1, d))


def _dn_conv_kernel(x_ref, w_ref, o_ref, *, ctx_len, n_qk_blocks):
    blk = pl.program_id(1)
    x = x_ref[0].astype(F32)
    t = x.shape[0]
    w = w_ref[...]
    row = lax.broadcasted_iota(jnp.int32, x.shape, 0)
    seg = row >= ctx_len
    half = DN_CONV_K // 2
    acc = x * w[half:half + 1, :]
    for tap in range(DN_CONV_K):
        dlt = tap - half
        if dlt == 0:
            continue
        shifted = pltpu.roll(x, (-dlt) % t, 0)
        src = row + dlt
        ok = (src >= 0) & (src < t) & ((src >= ctx_len) == seg)
        acc = acc + jnp.where(ok, shifted, 0.0) * w[tap:tap + 1, :]
    y = _silu(acc)
    inv = lax.rsqrt(jnp.sum(y * y, axis=-1, keepdims=True) + 1e-6)
    y = y * jnp.where(blk < n_qk_blocks, inv, 1.0)
    o_ref[0] = y.astype(BF16)


def _dn_conv(proj, conv_w_t, ctx_len, conv_ch, n_qk_blocks):
    b, t, _ = proj.shape
    return pl.pallas_call(
        functools.partial(_dn_conv_kernel, ctx_len=ctx_len, n_qk_blocks=n_qk_blocks),
        grid=(b, conv_ch // HEAD_DIM),
        in_specs=[pl.BlockSpec((1, t, HEAD_DIM), lambda bi, ci: (bi, 0, ci)),
                  pl.BlockSpec((DN_CONV_K, HEAD_DIM), lambda bi, ci: (0, ci))],
        out_specs=pl.BlockSpec((1, t, HEAD_DIM), lambda bi, ci: (bi, 0, ci)),
        out_shape=jax.ShapeDtypeStruct((b, t, conv_ch), BF16),
        compiler_params=_params(("parallel", "parallel")),
        name="dn_conv",
    )(proj, conv_w_t)


def _exact_f32_dot(tri, g):
    hi = g.astype(BF16)
    r1 = g - hi.astype(F32)
    mid = r1.astype(BF16)
    lo = (r1 - mid.astype(F32)).astype(BF16)
    return _dot(tri, hi) + _dot(tri, mid) + _dot(tri, lo)


def _dn_gate_kernel(ba_ref, alog_ref, dtb_ref, beta_ref, gc_ref, *, hv):
    ii = lax.broadcasted_iota(jnp.int32, (DN_CHUNK, DN_CHUNK), 0)
    jj = lax.broadcasted_iota(jnp.int32, (DN_CHUNK, DN_CHUNK), 1)
    lower = jnp.where(ii >= jj, 1.0, 0.0).astype(BF16)
    upper = jnp.where(ii <= jj, 1.0, 0.0).astype(BF16)
    ba = ba_ref[0]
    z = ba + dtb_ref[...]
    softplus = jnp.maximum(z, 0.0) + jnp.log1p(jnp.exp(-jnp.abs(z)))
    g = -jnp.exp(alog_ref[...]) * softplus
    beta = _sigmoid(ba)
    beta_ref[0, 0] = beta[:, 0:hv]
    beta_ref[0, 1] = beta[:, 2 * hv:3 * hv]
    for ci in range(ba.shape[0] // DN_CHUNK):
        rows = slice(ci * DN_CHUNK, (ci + 1) * DN_CHUNK)
        gc_ref[0, 0, rows, :] = _exact_f32_dot(lower, g[rows])[:, hv:2 * hv]
        gc_ref[0, 1, rows, :] = _exact_f32_dot(upper, g[rows])[:, 3 * hv:4 * hv]


def _dn_gates(ba, alog_row, dtb_row, hv):
    b, t, pw = ba.shape
    out = jax.ShapeDtypeStruct((b, 2, t, hv), F32)
    rows = next(r * DN_CHUNK for r in (4, 2, 1) if t % (r * DN_CHUNK) == 0)
    ospec = pl.BlockSpec((1, 2, rows, hv), lambda bi, ci: (bi, 0, ci, 0))
    return pl.pallas_call(
        functools.partial(_dn_gate_kernel, hv=hv),
        grid=(b, t // rows),
        in_specs=[pl.BlockSpec((1, rows, pw), lambda bi, ci: (bi, ci, 0)),
                  pl.BlockSpec((1, pw), lambda bi, ci: (0, 0)),
                  pl.BlockSpec((1, pw), lambda bi, ci: (0, 0))],
        out_specs=[ospec, ospec],
        out_shape=[out, out],
        compiler_params=_params(("parallel", "parallel")),
        name="dn_gates",
    )(ba, alog_row, dtb_row)


def _dn_chunk_kernel(q_ref, k_ref, v_ref, beta_ref, gcc_ref, gcr_ref, o_ref, s_ref, *, heads, scale):
    d = pl.program_id(2)
    p = pl.program_id(3)

    @pl.when(p == 0)
    def _():
        s_ref[...] = jnp.zeros_like(s_ref)

    c = DN_CHUNK
    ii = lax.broadcasted_iota(jnp.int32, (c, 2 * c), 0)
    lane = lax.broadcasted_iota(jnp.int32, (c, 2 * c), 1)
    jj = lane % c
    left = lane < c
    order = (ii - jj) * (1 - 2 * d)
    incl = order >= 0
    strict = order > 0
    eye = jnp.where(ii == jj, 1.0, 0.0)
    pair_masks = []
    sz = 1
    while sz < c:
        pair_masks.append((ii // (2 * sz) == jj // (2 * sz)) & (ii // sz != jj // sz))
        sz *= 2
    fwd = d == 0
    hr = range(heads)
    pr = range(heads // 2)
    lanes = lambda i: slice(i * HEAD_DIM, (i + 1) * HEAD_DIM)
    zeros_bf = jnp.zeros((c, HEAD_DIM), BF16)

    def block_diag(x):
        return jnp.concatenate([jnp.where(left, x, 0.0), jnp.where(left, 0.0, x)], axis=0).astype(BF16)

    def block_diag_wide(a, b):
        return jnp.concatenate([jnp.concatenate([a, zeros_bf], axis=1),
                                jnp.concatenate([zeros_bf, b], axis=1)], axis=0)

    kf, qf, kk, qk = [], [], [], []
    for m in pr:
        q = q_ref[0, :, lanes(m)]
        k = k_ref[0, :, lanes(m)]
        both = _dot_nt(jnp.concatenate([k, q], axis=0), jnp.concatenate([k, k], axis=0))
        kk.append(both[0:c])
        qk.append(both[c:2 * c] * scale)
        kf.append(k.astype(F32))
        qf.append(q.astype(F32))
    beta = [beta_ref[0, 0, 0, :, h:h + 1] for h in hr]
    gcol = [gcc_ref[0, 0, 0, :, h:h + 1] for h in hr]
    beta2 = [jnp.where(left, beta[2 * m], beta[2 * m + 1]) for m in pr]
    gcol2 = [jnp.where(left, gcol[2 * m], gcol[2 * m + 1]) for m in pr]
    grow2 = [gcr_ref[0, 0, 0, 0, m:m + 1, :] for m in pr]
    gam = [jnp.where(incl, jnp.exp(jnp.where(incl, gcol2[m] - grow2[m], 0.0)), 0.0) for m in pr]
    low = [jnp.where(strict, beta2[m] * kk[m] * gam[m], 0.0) for m in pr]
    tinv = [eye - jnp.where(pair_masks[0], low[m], 0.0) for m in pr]
    for pm in pair_masks[1:]:
        xc = [_dot(tinv[m].astype(BF16), block_diag(jnp.where(pm, low[m], 0.0))) for m in pr]
        tinv = [tinv[m] - _dot(xc[m].astype(BF16), block_diag(tinv[m])) for m in pr]
    tb = [tinv[m].astype(BF16) for m in pr]
    eg = [jnp.exp(gcol[h]) for h in hr]
    glast = [jnp.where(fwd, gcol[h][c - 1:c, :], gcol[h][0:1, :]) for h in hr]
    s_old = [s_ref[h] for h in hr]
    sb = [s_old[h].astype(BF16) for h in hr]
    ks_qs = [_dot(jnp.concatenate([(eg[h] * kf[h // 2]).astype(BF16),
                                   (qf[h // 2] * (scale * eg[h])).astype(BF16)], axis=0), sb[h]) for h in hr]
    rhs = [(beta[h] * (v_ref[0, :, lanes(h)].astype(F32) - ks_qs[h][0:c])).astype(BF16) for h in hr]
    vnb = [_dot(tb[m], block_diag_wide(rhs[2 * m], rhs[2 * m + 1])).astype(BF16) for m in pr]
    for m in pr:
        intra = _dot((qk[m] * gam[m]).astype(BF16),
                     block_diag_wide(vnb[m][:, 0:HEAD_DIM], vnb[m][:, HEAD_DIM:2 * HEAD_DIM]))
        inter = jnp.concatenate([ks_qs[2 * m][c:2 * c], ks_qs[2 * m + 1][c:2 * c]], axis=1)
        o_ref[0, 0, :, 2 * m * HEAD_DIM:(2 * m + 2) * HEAD_DIM] = (inter + intra).astype(BF16)
    for h in hr:
        k_dec = (kf[h // 2] * jnp.exp(glast[h] - gcol[h])).astype(BF16)
        vn = vnb[h // 2][:, (h % 2) * HEAD_DIM:(h % 2 + 1) * HEAD_DIM]
        s_ref[h] = s_old[h] * jnp.exp(glast[h]) + _dot_tn(k_dec, vn)


def _dn_chunks(qkv, beta_g, gcc_g, gcr_g, ctx_len, qk_w, hv):
    b, t, _ = qkv.shape
    g = beta_g.shape[-1]
    assert hv % g == 0 and qk_w % (g // 2 * HEAD_DIM) == 0
    nc = t // DN_CHUNK
    nctx = ctx_len // DN_CHUNK
    qw = g // 2 * HEAD_DIM
    vw = g * HEAD_DIM
    k_off = qk_w // qw
    v_off = 2 * qk_w // vw

    def chunk(d, p):
        back = jnp.where(p < nctx, nctx - 1 - p, nc + nctx - 1 - p)
        return jnp.where(d == 0, p, back)

    return pl.pallas_call(
        functools.partial(_dn_chunk_kernel, heads=g, scale=HEAD_DIM ** -0.5),
        grid=(b, hv // g, 2, nc),
        in_specs=[pl.BlockSpec((1, DN_CHUNK, qw), lambda bi, gi, d, p: (bi, chunk(d, p), gi)),
                  pl.BlockSpec((1, DN_CHUNK, qw), lambda bi, gi, d, p: (bi, chunk(d, p), k_off + gi)),
                  pl.BlockSpec((1, DN_CHUNK, vw), lambda bi, gi, d, p: (bi, chunk(d, p), v_off + gi)),
                  pl.BlockSpec((1, 1, 1, DN_CHUNK, g), lambda bi, gi, d, p: (bi, d, gi, chunk(d, p), 0)),
                  pl.BlockSpec((1, 1, 1, DN_CHUNK, g), lambda bi, gi, d, p: (bi, d, gi, chunk(d, p), 0)),
                  pl.BlockSpec((1, 1, 1, 1, g // 2, 2 * DN_CHUNK),
                               lambda bi, gi, d, p: (bi, d, gi, chunk(d, p), 0, 0))],
        out_specs=pl.BlockSpec((1, 1, DN_CHUNK, vw), lambda bi, gi, d, p: (bi, d, chunk(d, p), gi)),
        out_shape=jax.ShapeDtypeStruct((b, 2, t, hv * HEAD_DIM), BF16),
        scratch_shapes=[pltpu.VMEM((g, HEAD_DIM, HEAD_DIM), F32)],
        compiler_params=_params(("parallel", "parallel", "arbitrary", "arbitrary")),
        name="dn_chunks",
    )(qkv, qkv, qkv, beta_g, gcc_g, gcr_g)


def _dn_finish_kernel(o_ref, z_ref, w_ref, y_ref):
    cw = y_ref.shape[2]
    for h in range(cw // HEAD_DIM):
        hl = slice(h * HEAD_DIM, (h + 1) * HEAD_DIM)
        o = o_ref[0, 0, :, hl].astype(F32) + o_ref[0, 1, :, hl].astype(F32)
        ms = jnp.mean(o * o, axis=-1, keepdims=True)
        y = o * lax.rsqrt(ms + RMS_EPS) * w_ref[...]
        y_ref[0, :, hl] = (y * _silu(z_ref[0, :, hl].astype(F32))).astype(BF16)


def _dn_finish(o, proj, onorm_w, z_off):
    b, _, t, vw = o.shape
    cw = 1024 if vw % 1024 == 0 else vw
    assert z_off % cw == 0
    tr = math.gcd(t, 256)
    return pl.pallas_call(
        _dn_finish_kernel,
        grid=(b, t // tr, vw // cw),
        in_specs=[pl.BlockSpec((1, 2, tr, cw), lambda bi, ri, ci: (bi, 0, ri, ci)),
                  pl.BlockSpec((1, tr, cw), lambda bi, ri, ci: (bi, ri, z_off // cw + ci)),
                  pl.BlockSpec((1, HEAD_DIM), lambda bi, ri, ci: (0, 0))],
        out_specs=pl.BlockSpec((1, tr, cw), lambda bi, ri, ci: (bi, ri, ci)),
        out_shape=jax.ShapeDtypeStruct((b, t, vw), BF16),
        compiler_params=_params(("parallel", "parallel", "parallel")),
        name="dn_finish",
    )(o, proj, onorm_w.reshape(1, HEAD_DIM))


def _deltanet(proj, ba, conv_w, a_log, dt_bias, onorm_w, ctx_len, d_model):
    hk = d_model // HEAD_DIM
    hv = 2 * hk
    qk_w = hk * HEAD_DIM
    conv_ch = 2 * qk_w + hv * HEAD_DIM
    b, t, _ = proj.shape
    g = min(DN_HEADS_PER_STEP, hv)
    qkv = _dn_conv(proj, jnp.transpose(conv_w), ctx_len, conv_ch, 2 * hk)
    zeros = jnp.zeros((hv,), F32)
    alog_row = jnp.concatenate([zeros, a_log[0], zeros, a_log[1]]).reshape(1, 4 * hv)
    dtb_row = jnp.concatenate([zeros, dt_bias[0], zeros, dt_bias[1]]).reshape(1, 4 * hv)
    beta, gc = _dn_gates(ba, alog_row, dtb_row, hv)
    nc = t // DN_CHUNK
    beta_g = beta.reshape(b, 2, t, hv // g, g).transpose(0, 1, 3, 2, 4)
    gcc_g = gc.reshape(b, 2, t, hv // g, g).transpose(0, 1, 3, 2, 4)
    gcr_g = (gc.reshape(b, 2, nc, DN_CHUNK, hv // g, g).transpose(0, 1, 4, 2, 5, 3)
             .reshape(b, 2, hv // g, nc, g // 2, 2 * DN_CHUNK))
    o = _dn_chunks(qkv, beta_g, gcc_g, gcr_g, ctx_len, qk_w, hv)
    return _dn_finish(o, proj, onorm_w, conv_ch)


def _rope(x, cos, sin):
    quarter = HEAD_DIM // 4
    lane = lax.broadcasted_iota(jnp.int32, x.shape, 1)
    up = pltpu.roll(x, HEAD_DIM - quarter, 1)
    down = pltpu.roll(x, quarter, 1)
    rot = jnp.where((lane % (2 * quarter)) < quarter, -up, down)
    return x * cos + rot * sin


def _da_kernel(q_ref, k_ref, v_ref, g_ref, cos_ref, sin_ref, lam_ref, sw_ref, o_ref,
               krot_ref, q_scr, m_ref, l_ref, acc_ref, *, ctx_len, lambda_init, scale, ck):
    qi = pl.program_id(2)
    t = k_ref.shape[1]
    tq = q_ref.shape[1]
    maps = range(2)
    lanes = lambda j: slice(j * HEAD_DIM, (j + 1) * HEAD_DIM)

    @pl.when(qi == 0)
    def _():
        for j in maps:
            kk = k_ref[0, :, lanes(j)].astype(F32)
            krot_ref[:, lanes(j)] = _rope(kk, cos_ref[...], sin_ref[...]).astype(BF16)

    qs = pl.multiple_of(qi * tq, tq)
    cos_q = cos_ref[pl.ds(qs, tq), :]
    sin_q = sin_ref[pl.ds(qs, tq), :]
    for j in maps:
        q = _rope(q_ref[0, :, lanes(j)].astype(F32), cos_q, sin_q)
        q_scr[j] = (q * (scale * math.log2(math.e))).astype(BF16)

    stat = (tq, HEAD_DIM)

    def widen(x, width):
        if width % HEAD_DIM == 0:
            return pltpu.repeat(x, width // HEAD_DIM, 1)
        return jnp.broadcast_to(x[:, 0:1], (tq, width))

    for j in maps:
        s = _dot_nt(q_scr[j], krot_ref[0:ctx_len, lanes(j)])
        m = jnp.broadcast_to(jnp.max(s, axis=-1, keepdims=True), stat)
        pr = jnp.exp2(s - widen(m, ctx_len))
        m_ref[j] = m
        l_ref[j] = jnp.broadcast_to(jnp.sum(pr, axis=-1, keepdims=True), stat)
        acc_ref[j] = _dot(pr.astype(BF16), v_ref[0, 0:ctx_len, :])

    @pl.when(qi > 0)
    def _():
        m = [m_ref[j] for j in maps]
        l = [l_ref[j] for j in maps]
        acc = [acc_ref[j] for j in maps]
        for c in range((t - ctx_len) // ck):
            k0 = ctx_len + c * ck
            s = [_dot_nt(q_scr[j], krot_ref[k0:k0 + ck, lanes(j)]) for j in maps]
            m_new = [jnp.maximum(m[j], jnp.broadcast_to(jnp.max(s[j], axis=-1, keepdims=True), stat))
                     for j in maps]
            alpha = [jnp.exp2(m[j] - m_new[j]) for j in maps]
            pr = [jnp.exp2(s[j] - widen(m_new[j], ck)) for j in maps]
            v = v_ref[0, k0:k0 + ck, :]
            l = [alpha[j] * l[j] + jnp.broadcast_to(jnp.sum(pr[j], axis=-1, keepdims=True), stat) for j in maps]
            acc = [widen(alpha[j], 2 * HEAD_DIM) * acc[j] + _dot(pr[j].astype(BF16), v) for j in maps]
            m = m_new
        for j in maps:
            l_ref[j] = l[j]
            acc_ref[j] = acc[j]

    lam = lam_ref[...]
    lam_full = (jnp.exp(jnp.sum(lam[0:1] * lam[1:2], axis=-1, keepdims=True))
                - jnp.exp(jnp.sum(lam[2:3] * lam[3:4], axis=-1, keepdims=True)) + lambda_init)
    o = (acc_ref[0] / widen(l_ref[0], 2 * HEAD_DIM)
         - lam_full * (acc_ref[1] / widen(l_ref[1], 2 * HEAD_DIM)))
    ms = jnp.mean(o * o, axis=-1, keepdims=True)
    y = o * lax.rsqrt(ms + DA_SUBLN_EPS) * sw_ref[...] * (1.0 - lambda_init)
    o_ref[0] = (y * _silu(g_ref[0].astype(F32))).astype(BF16)


def _rope_tables(ctx_len, n):
    tok = jnp.arange(n)
    row = (tok // GRID_W).astype(F32)
    col = (tok % GRID_W).astype(F32)
    half = HEAD_DIM // 2
    inv = ROPE_THETA ** (-jnp.arange(0, half, 2, dtype=F32) / half)
    ang_r = row[:, None] * inv
    ang_c = col[:, None] * inv
    ang = jnp.concatenate([ang_r, ang_r, ang_c, ang_c], axis=-1)
    cos = jnp.concatenate([jnp.ones((ctx_len, HEAD_DIM), F32), jnp.cos(ang)], axis=0)
    sin = jnp.concatenate([jnp.zeros((ctx_len, HEAD_DIM), F32), jnp.sin(ang)], axis=0)
    return cos, sin


def _diff_attn(proj, lam, subln_w, ctx_len, d_model, layer_idx):
    b, t, _ = proj.shape
    hw = 2 * HEAD_DIM
    heads = d_model // hw
    tq = ctx_len
    assert t % tq == 0
    lambda_init = 0.8 - 0.6 * math.exp(-0.3 * layer_idx)
    cos, sin = _rope_tables(ctx_len, t - ctx_len)
    full = lambda bi, hi, qi: (0, 0)
    ck = min(DA_KEY_CHUNK, t - ctx_len)
    assert (t - ctx_len) % ck == 0
    return pl.pallas_call(
        functools.partial(_da_kernel, ctx_len=ctx_len, lambda_init=lambda_init, scale=HEAD_DIM ** -0.5, ck=ck),
        grid=(b, heads, t // tq),
        in_specs=[pl.BlockSpec((1, tq, hw), lambda bi, hi, qi: (bi, qi, hi)),
                  pl.BlockSpec((1, t, hw), lambda bi, hi, qi: (bi, 0, heads + hi)),
                  pl.BlockSpec((1, t, hw), lambda bi, hi, qi: (bi, 0, 2 * heads + hi)),
                  pl.BlockSpec((1, tq, hw), lambda bi, hi, qi: (bi, qi, 3 * heads + hi)),
                  pl.BlockSpec((t, HEAD_DIM), full),
                  pl.BlockSpec((t, HEAD_DIM), full),
                  pl.BlockSpec((4, HEAD_DIM), full),
                  pl.BlockSpec((1, hw), full)],
        out_specs=pl.BlockSpec((1, tq, hw), lambda bi, hi, qi: (bi, qi, hi)),
        out_shape=jax.ShapeDtypeStruct((b, t, heads * hw), BF16),
        scratch_shapes=[pltpu.VMEM((t, hw), BF16),
                        pltpu.VMEM((2, tq, HEAD_DIM), BF16),
                        pltpu.VMEM((2, tq, HEAD_DIM), F32),
                        pltpu.VMEM((2, tq, HEAD_DIM), F32),
                        pltpu.VMEM((2, tq, hw), F32)],
        compiler_params=_params(("parallel", "parallel", "arbitrary")),
        name="diff_attn",
    )(proj, proj, proj, proj, cos, sin, lam, subln_w.reshape(1, hw))


def _na_kernel(q_ref, k_ref, v_ref, g_ref, bias_ref, o_ref, *, ctx_len, rows, scale):
    wk = NA_WR * GRID_W
    kc = k_ref[0, 0:ctx_len, :]
    vc = v_ref[0, 0:ctx_len, :]

    s = _dot_nt(q_ref[0, 0:ctx_len, :], kc) * scale
    m = jnp.max(s, axis=-1, keepdims=True)
    pr = jnp.exp(s - m)
    o = _dot(pr.astype(BF16), vc) / jnp.sum(pr, axis=-1, keepdims=True)
    o_ref[0, 0:ctx_len, :] = (o * _silu(g_ref[0, 0:ctx_len, :].astype(F32))).astype(BF16)

    def body(it, carry):
        nr = range(NA_ROWS_PER_ITER)
        r = [it * NA_ROWS_PER_ITER + i for i in nr]
        rs = [jnp.clip(r[i] - NA_WR // 2, 0, rows - NA_WR) for i in nr]
        qs = [pl.multiple_of(ctx_len + r[i] * GRID_W, GRID_W) for i in nr]
        ks = [pl.multiple_of(ctx_len + rs[i] * GRID_W, GRID_W) for i in nr]
        q = [q_ref[0, pl.ds(qs[i], GRID_W), :] for i in nr]
        s_win = [_dot_nt(q[i], k_ref[0, pl.ds(ks[i], wk), :]) * scale + bias_ref[0, rs[i] - r[i] + NA_WR - 1]
                 for i in nr]
        s_ctx = [_dot_nt(q[i], kc) * scale for i in nr]
        mx = [jnp.maximum(jnp.max(s_win[i], axis=-1, keepdims=True), jnp.max(s_ctx[i], axis=-1, keepdims=True))
              for i in nr]
        p_win = [jnp.exp(s_win[i] - mx[i]) for i in nr]
        p_ctx = [jnp.exp(s_ctx[i] - mx[i]) for i in nr]
        den = [jnp.sum(p_win[i], axis=-1, keepdims=True) + jnp.sum(p_ctx[i], axis=-1, keepdims=True) for i in nr]
        out = [(_dot(p_win[i].astype(BF16), v_ref[0, pl.ds(ks[i], wk), :]) + _dot(p_ctx[i].astype(BF16), vc)) / den[i]
               for i in nr]
        for i in nr:
            gate = g_ref[0, pl.ds(qs[i], GRID_W), :].astype(F32)
            o_ref[0, pl.ds(qs[i], GRID_W), :] = (out[i] * _silu(gate)).astype(BF16)
        return carry

    lax.fori_loop(0, rows // NA_ROWS_PER_ITER, body, 0)


def _na_bias_table(rpb):
    heads = rpb.shape[0]
    cols = jnp.arange(GRID_W)
    start = jnp.clip(cols - NA_WC // 2, 0, GRID_W - NA_WC)
    inside = (cols[None, :] >= start[:, None]) & (cols[None, :] < start[:, None] + NA_WC)
    dc = cols[None, :] - cols[:, None] + NA_WC - 1
    rpb = rpb.astype(F32)
    wide = jnp.full((heads, 2 * NA_WR - 1, GRID_W, GRID_W), NEG_BIG, F32)
    for off in range(2 * NA_WC - 1):
        wide = jnp.where((inside & (dc == off))[None, None], rpb[:, :, off, None, None], wide)
    tab = jnp.stack([wide[:, o:o + NA_WR] for o in range(NA_WR)], axis=1)
    return tab.transpose(0, 1, 3, 2, 4).reshape(heads, NA_WR, GRID_W, NA_WR * GRID_W)


def _neighbourhood(proj, rpb, ctx_len, d_model):
    b, t, _ = proj.shape
    heads = d_model // HEAD_DIM
    rows = (t - ctx_len) // GRID_W
    assert rows >= NA_WR
    wk = NA_WR * GRID_W
    bias = _na_bias_table(rpb)
    col = lambda off: (lambda bi, hi: (bi, 0, off * heads + hi))
    return pl.pallas_call(
        functools.partial(_na_kernel, ctx_len=ctx_len, rows=rows, scale=HEAD_DIM ** -0.5),
        grid=(b, heads),
        in_specs=[pl.BlockSpec((1, t, HEAD_DIM), col(0)),
                  pl.BlockSpec((1, t, HEAD_DIM), col(1)),
                  pl.BlockSpec((1, t, HEAD_DIM), col(2)),
                  pl.BlockSpec((1, t, HEAD_DIM), col(3)),
                  pl.BlockSpec((1, NA_WR, GRID_W, wk), lambda bi, hi: (hi, 0, 0, 0))],
        out_specs=pl.BlockSpec((1, t, HEAD_DIM), col(0)),
        out_shape=jax.ShapeDtypeStruct((b, t, heads * HEAD_DIM), BF16),
        compiler_params=_params(("parallel", "parallel")),
        name="neighbourhood",
    )(proj, proj, proj, proj, bias)


def kernel(x, c, ctx, c_ctx, norm_w, ada_w, ada_b, dn_w_in, dn_conv_w, dn_a_log, dn_dt_bias, dn_onorm_w,
           dn_w_out, da_w_in, da_lambda, da_subln_w, da_w_out, na_w_in, na_rpb, na_w_out, final_norm_w):
    b, n, d = x.shape
    ctx_len = ctx.shape[1]
    t = ctx_len + n
    depth = norm_w.shape[0]
    assert b + 1 <= MOD_ROWS and n % GRID_W == 0 and ctx_len % DN_CHUNK == 0
    tm = t // 4
    assert t % 4 == 0 and tm % 16 == 0

    xa = jnp.concatenate([ctx, x], axis=1)
    cvec = jnp.concatenate([c, c_ctx[None, :], jnp.zeros((MOD_ROWS - b - 1, d), F32)], axis=0)
    mods = _modulation(cvec, ada_w, ada_b)

    conv_ch = 4 * d
    for i in range(depth):
        kind, j = i % N_MIXERS, i // N_MIXERS
        sh, sc, gt = mods[i, :, 0:d], mods[i, :, d:2 * d], mods[i, :, 2 * d:3 * d]
        lat = lambda m: m[0:b, None, :]
        cx = lambda m: m[b:b + 1, :]
        if kind == 0:
            w_in = dn_w_in[j]
            proj, ba = _norm_proj(xa, norm_w[i], lat(sc), lat(sh), cx(sc), cx(sh),
                                  w_in[:, 0:conv_ch + 2 * d].astype(BF16), w_in[:, conv_ch + 2 * d:].astype(BF16),
                                  ctx_len, tm)
            y = _deltanet(proj, ba, dn_conv_w[j], dn_a_log[j], dn_dt_bias[j], dn_onorm_w[j], ctx_len, d)
            w_out = dn_w_out[j]
        elif kind == 1:
            proj, _ = _norm_proj(xa, norm_w[i], lat(sc), lat(sh), cx(sc), cx(sh),
                                 da_w_in[j].astype(BF16), None, ctx_len, tm)
            y = _diff_attn(proj, da_lambda[j], da_subln_w[j], ctx_len, d, i)
            w_out = da_w_out[j]
        else:
            proj, _ = _norm_proj(xa, norm_w[i], lat(sc), lat(sh), cx(sc), cx(sh),
                                 na_w_in[j].astype(BF16), None, ctx_len, tm)
            y = _neighbourhood(proj, na_rpb[j], ctx_len, d)
            w_out = na_w_out[j]
        xa = _out_proj(y, w_out.astype(BF16), xa, lat(gt), cx(gt), ctx_len, tm)
    return _final_norm(xa, final_norm_w, ctx_len)
```

```python
import functools
import math

import jax
import jax.numpy as jnp
from jax import lax
from jax.experimental import pallas as pl
from jax.experimental.pallas import tpu as pltpu

F32 = jnp.float32
BF16 = jnp.bfloat16

GRID_W = 64
N_MIXERS = 3
RMS_EPS = 1e-6
HEAD_DIM = 128
DN_CONV_K = 5
DN_CHUNK = 64
DN_HEADS_PER_STEP = 32
DA_SUBLN_EPS = 1e-5
DA_QUERY_TILE = 1024
DA_KEY_CHUNK = 512
ROPE_THETA = 10000.0
NA_WR = 8
NA_WC = 16
NA_ROWS_PER_ITER = 8
NEG_BIG = -1e30
PROJ_TN = 512
MOD_ROWS = 8
V7X_VMEM_LIMIT = 56 * 1024 * 1024


def _sigmoid(x):
    return 1.0 / (1.0 + jnp.exp(-x))


def _silu(x):
    return x * _sigmoid(x)


def _dot(a, b):
    return jnp.dot(a, b, preferred_element_type=F32)


def _dot_nt(a, b):
    return lax.dot_general(a, b, (((1,), (1,)), ((), ())), preferred_element_type=F32)


def _dot_tn(a, b):
    return lax.dot_general(a, b, (((0,), (0,)), ((), ())), preferred_element_type=F32)


def _params(semantics, vmem=V7X_VMEM_LIMIT):
    return pltpu.CompilerParams(dimension_semantics=semantics, vmem_limit_bytes=vmem)


def _mod_kernel(c_ref, w_ref, b_ref, o_ref):
    s = _silu(c_ref[...])
    o_ref[0] = jnp.dot(s, w_ref[0], preferred_element_type=F32,
                       precision=lax.Precision.HIGHEST) + b_ref[0]


def _modulation(cvec, ada_w, ada_b):
    depth, d, p = ada_w.shape
    tn = 1024 if p % 1024 == 0 else p
    return pl.pallas_call(
        _mod_kernel,
        grid=(depth, p // tn),
        in_specs=[pl.BlockSpec((MOD_ROWS, d), lambda i, n: (0, 0)),
                  pl.BlockSpec((1, d, tn), lambda i, n: (i, 0, n)),
                  pl.BlockSpec((1, 1, tn), lambda i, n: (i, 0, n))],
        out_specs=pl.BlockSpec((1, MOD_ROWS, tn), lambda i, n: (i, 0, n)),
        out_shape=jax.ShapeDtypeStruct((depth, MOD_ROWS, p), F32),
        compiler_params=_params(("parallel", "parallel")),
        name="adaln_mod",
    )(cvec, ada_w, ada_b.reshape(depth, 1, p))


def _norm_proj_kernel(*refs, ctx_len, tm, has_extra):
    if has_extra:
        x_ref, nw_ref, scl_ref, shl_ref, scc_ref, shc_ref, w_ref, wx_ref, o_ref, ox_ref, h_ref = refs
    else:
        x_ref, nw_ref, scl_ref, shl_ref, scc_ref, shc_ref, w_ref, o_ref, h_ref = refs
    mi = pl.program_id(1)
    ni = pl.program_id(2)

    @pl.when(ni == 0)
    def _():
        x = x_ref[0]
        ms = jnp.mean(x * x, axis=-1, keepdims=True)
        y = x * lax.rsqrt(ms + RMS_EPS) * nw_ref[...]
        row = lax.broadcasted_iota(jnp.int32, (tm, 1), 0) + mi * tm
        is_ctx = row < ctx_len
        sc = jnp.where(is_ctx, scc_ref[...], scl_ref[0])
        sh = jnp.where(is_ctx, shc_ref[...], shl_ref[0])
        h = (y * (1.0 + sc) + sh).astype(BF16)
        h_ref[...] = h
        if has_extra:
            ox_ref[0] = _dot(h, wx_ref[...])

    o_ref[0] = _dot(h_ref[...], w_ref[...]).astype(BF16)


def _norm_proj(xa, nw, scl, shl, scc, shc, w, wx, ctx_len, tm):
    b, t, d = xa.shape
    p = w.shape[1]
    tn = PROJ_TN if p % PROJ_TN == 0 else PROJ_TN // 2
    assert t % tm == 0 and p % tn == 0
    has_extra = wx is not None
    vec_l = pl.BlockSpec((1, 1, d), lambda bi, mi, ni: (bi, 0, 0))
    vec_c = pl.BlockSpec((1, d), lambda bi, mi, ni: (0, 0))
    in_specs = [pl.BlockSpec((1, tm, d), lambda bi, mi, ni: (bi, mi, 0)),
                vec_c, vec_l, vec_l, vec_c, vec_c,
                pl.BlockSpec((d, tn), lambda bi, mi, ni: (0, ni))]
    out_specs = [pl.BlockSpec((1, tm, tn), lambda bi, mi, ni: (bi, mi, ni))]
    out_shape = [jax.ShapeDtypeStruct((b, t, p), BF16)]
    args = [xa, nw.reshape(1, d), scl, shl, scc, shc, w]
    if has_extra:
        px = wx.shape[1]
        in_specs.append(pl.BlockSpec((d, px), lambda bi, mi, ni: (0, 0)))
        out_specs.append(pl.BlockSpec((1, tm, px), lambda bi, mi, ni: (bi, mi, 0)))
        out_shape.append(jax.ShapeDtypeStruct((b, t, px), F32))
        args.append(wx)
    outs = pl.pallas_call(
        functools.partial(_norm_proj_kernel, ctx_len=ctx_len, tm=tm, has_extra=has_extra),
        grid=(b, t // tm, p // tn),
        in_specs=in_specs,
        out_specs=out_specs,
        out_shape=out_shape,
        scratch_shapes=[pltpu.VMEM((tm, d), BF16)],
        compiler_params=_params(("parallel", "parallel", "arbitrary")),
        name="norm_proj",
    )(*args)
    return (outs[0], outs[1]) if has_extra else (outs[0], None)


def _out_proj_kernel(y_ref, w_ref, x_ref, gtl_ref, gtc_ref, o_ref, *, ctx_len, tm):
    mi = pl.program_id(1)
    acc = _dot(y_ref[0], w_ref[...])
    row = lax.broadcasted_iota(jnp.int32, (tm, 1), 0) + mi * tm
    gt = jnp.where(row < ctx_len, gtc_ref[...], gtl_ref[0])
    o_ref[0] = x_ref[0] + gt * acc


def _out_proj(y, w, xa, gtl, gtc, ctx_len, tm):
    b, t, d = xa.shape
    kd = y.shape[2]
    tn = PROJ_TN if d % PROJ_TN == 0 else PROJ_TN // 2
    while 2 * (tm * kd * 2 + kd * tn * 2 + 2 * tm * tn * 4) > V7X_VMEM_LIMIT * 3 // 4:
        tn //= 2
    assert d % tn == 0 and t % tm == 0
    return pl.pallas_call(
        functools.partial(_out_proj_kernel, ctx_len=ctx_len, tm=tm),
        grid=(b, t // tm, d // tn),
        in_specs=[pl.BlockSpec((1, tm, kd), lambda bi, mi, ni: (bi, mi, 0)),
                  pl.BlockSpec((kd, tn), lambda bi, mi, ni: (0, ni)),
                  pl.BlockSpec((1, tm, tn), lambda bi, mi, ni: (bi, mi, ni)),
                  pl.BlockSpec((1, 1, tn), lambda bi, mi, ni: (bi, 0, ni)),
                  pl.BlockSpec((1, tn), lambda bi, mi, ni: (0, ni))],
        out_specs=pl.BlockSpec((1, tm, tn), lambda bi, mi, ni: (bi, mi, ni)),
        out_shape=jax.ShapeDtypeStruct((b, t, d), F32),
        compiler_params=_params(("parallel", "parallel", "arbitrary")),
        name="out_proj",
    )(y, w, xa, gtl, gtc)


def _final_norm_kernel(x_ref, w_ref, o_ref):
    x = x_ref[0]
    ms = jnp.mean(x * x, axis=-1, keepdims=True)
    o_ref[0] = x * lax.rsqrt(ms + RMS_EPS) * w_ref[...]


def _final_norm(xa, w, ctx_len):
    b, t, d = xa.shape
    n = t - ctx_len
    tr = math.gcd(ctx_len, 256)
    assert n % tr == 0
    off = ctx_len // tr
    return pl.pallas_call(
        _final_norm_kernel,
        grid=(b, n // tr),
        in_specs=[pl.BlockSpec((1, tr, d), lambda bi, ri: (bi, ri + off, 0)),
                  pl.BlockSpec((1, d), lambda bi, ri: (0, 0))],
        out_specs=pl.BlockSpec((1, tr, d), lambda bi, ri: (bi, ri, 0)),
        out_shape=jax.ShapeDtypeStruct((b, n, d), F32),
        compiler_params=_params(("parallel", "parallel")),
        name="final_norm",
    )(xa, w.reshape(1, d))


def _dn_conv_kernel(x_ref, w_ref, o_ref, *, ctx_len, n_qk_blocks):
    blk = pl.program_id(1)
    x = x_ref[0].astype(F32)
    t = x.shape[0]
    w = w_ref[...]
    row = lax.broadcasted_iota(jnp.int32, x.shape, 0)
    seg = row >= ctx_len
    half = DN_CONV_K // 2
    acc = x * w[half:half + 1, :]
    for tap in range(DN_CONV_K):
        dlt = tap - half
        if dlt == 0:
            continue
        shifted = pltpu.roll(x, (-dlt) % t, 0)
        src = row + dlt
        ok = (src >= 0) & (src < t) & ((src >= ctx_len) == seg)
        acc = acc + jnp.where(ok, shifted, 0.0) * w[tap:tap + 1, :]
    y = _silu(acc)
    inv = lax.rsqrt(jnp.sum(y * y, axis=-1, keepdims=True) + 1e-6)
    y = y * jnp.where(blk < n_qk_blocks, inv, 1.0)
    o_ref[0] = y.astype(BF16)


def _dn_conv(proj, conv_w_t, ctx_len, conv_ch, n_qk_blocks):
    b, t, _ = proj.shape
    return pl.pallas_call(
        functools.partial(_dn_conv_kernel, ctx_len=ctx_len, n_qk_blocks=n_qk_blocks),
        grid=(b, conv_ch // HEAD_DIM),
        in_specs=[pl.BlockSpec((1, t, HEAD_DIM), lambda bi, ci: (bi, 0, ci)),
                  pl.BlockSpec((DN_CONV_K, HEAD_DIM), lambda bi, ci: (0, ci))],
        out_specs=pl.BlockSpec((1, t, HEAD_DIM), lambda bi, ci: (bi, 0, ci)),
        out_shape=jax.ShapeDtypeStruct((b, t, conv_ch), BF16),
        compiler_params=_params(("parallel", "parallel")),
        name="dn_conv",
    )(proj, conv_w_t)


def _exact_f32_dot(tri, g):
    hi = g.astype(BF16)
    r1 = g - hi.astype(F32)
    mid = r1.astype(BF16)
    lo = (r1 - mid.astype(F32)).astype(BF16)
    return _dot(tri, hi) + _dot(tri, mid) + _dot(tri, lo)


def _dn_gate_kernel(ba_ref, alog_ref, dtb_ref, beta_ref, gc_ref, *, hv):
    ii = lax.broadcasted_iota(jnp.int32, (DN_CHUNK, DN_CHUNK), 0)
    jj = lax.broadcasted_iota(jnp.int32, (DN_CHUNK, DN_CHUNK), 1)
    lower = jnp.where(ii >= jj, 1.0, 0.0).astype(BF16)
    upper = jnp.where(ii <= jj, 1.0, 0.0).astype(BF16)
    ba = ba_ref[0]
    z = ba + dtb_ref[...]
    softplus = jnp.maximum(z, 0.0) + jnp.log1p(jnp.exp(-jnp.abs(z)))
    g = -jnp.exp(alog_ref[...]) * softplus
    beta = _sigmoid(ba)
    beta_ref[0, 0] = beta[:, 0:hv]
    beta_ref[0, 1] = beta[:, 2 * hv:3 * hv]
    for ci in range(ba.shape[0] // DN_CHUNK):
        rows = slice(ci * DN_CHUNK, (ci + 1) * DN_CHUNK)
        gc_ref[0, 0, rows, :] = _exact_f32_dot(lower, g[rows])[:, hv:2 * hv]
        gc_ref[0, 1, rows, :] = _exact_f32_dot(upper, g[rows])[:, 3 * hv:4 * hv]


def _dn_gates(ba, alog_row, dtb_row, hv):
    b, t, pw = ba.shape
    out = jax.ShapeDtypeStruct((b, 2, t, hv), F32)
    rows = next(r * DN_CHUNK for r in (4, 2, 1) if t % (r * DN_CHUNK) == 0)
    ospec = pl.BlockSpec((1, 2, rows, hv), lambda bi, ci: (bi, 0, ci, 0))
    return pl.pallas_call(
        functools.partial(_dn_gate_kernel, hv=hv),
        grid=(b, t // rows),
        in_specs=[pl.BlockSpec((1, rows, pw), lambda bi, ci: (bi, ci, 0)),
                  pl.BlockSpec((1, pw), lambda bi, ci: (0, 0)),
                  pl.BlockSpec((1, pw), lambda bi, ci: (0, 0))],
        out_specs=[ospec, ospec],
        out_shape=[out, out],
        compiler_params=_params(("parallel", "parallel")),
        name="dn_gates",
    )(ba, alog_row, dtb_row)


def _dn_chunk_kernel(q_ref, k_ref, v_ref, beta_ref, gcc_ref, gcr_ref, o_ref, s_ref, *, heads, scale):
    d = pl.program_id(2)
    p = pl.program_id(3)

    @pl.when(p == 0)
    def _():
        s_ref[...] = jnp.zeros_like(s_ref)

    c = DN_CHUNK
    ii = lax.broadcasted_iota(jnp.int32, (c, 2 * c), 0)
    lane = lax.broadcasted_iota(jnp.int32, (c, 2 * c), 1)
    jj = lane % c
    left = lane < c
    order = (ii - jj) * (1 - 2 * d)
    incl = order >= 0
    strict = order > 0
    eye = jnp.where(ii == jj, 1.0, 0.0)
    pair_masks = []
    sz = 1
    while sz < c:
        pair_masks.append((ii // (2 * sz) == jj // (2 * sz)) & (ii // sz != jj // sz))
        sz *= 2
    fwd = d == 0
    hr = range(heads)
    pr = range(heads // 2)
    lanes = lambda i: slice(i * HEAD_DIM, (i + 1) * HEAD_DIM)
    zeros_bf = jnp.zeros((c, HEAD_DIM), BF16)

    def block_diag(x):
        return jnp.concatenate([jnp.where(left, x, 0.0), jnp.where(left, 0.0, x)], axis=0).astype(BF16)

    def block_diag_wide(a, b):
        return jnp.concatenate([jnp.concatenate([a, zeros_bf], axis=1),
                                jnp.concatenate([zeros_bf, b], axis=1)], axis=0)

    kf, qf, kk, qk = [], [], [], []
    for m in pr:
        q = q_ref[0, :, lanes(m)]
        k = k_ref[0, :, lanes(m)]
        both = _dot_nt(jnp.concatenate([k, q], axis=0), jnp.concatenate([k, k], axis=0))
        kk.append(both[0:c])
        qk.append(both[c:2 * c] * scale)
        kf.append(k.astype(F32))
        qf.append(q.astype(F32))
    beta = [beta_ref[0, 0, 0, :, h:h + 1] for h in hr]
    gcol = [gcc_ref[0, 0, 0, :, h:h + 1] for h in hr]
    beta2 = [jnp.where(left, beta[2 * m], beta[2 * m + 1]) for m in pr]
    gcol2 = [jnp.where(left, gcol[2 * m], gcol[2 * m + 1]) for m in pr]
    grow2 = [gcr_ref[0, 0, 0, 0, m:m + 1, :] for m in pr]
    gam = [jnp.where(incl, jnp.exp(jnp.where(incl, gcol2[m] - grow2[m], 0.0)), 0.0) for m in pr]
    low = [jnp.where(strict, beta2[m] * kk[m] * gam[m], 0.0) for m in pr]
    tinv = [eye - jnp.where(pair_masks[0], low[m], 0.0) for m in pr]
    for pm in pair_masks[1:]:
        xc = [_dot(tinv[m].astype(BF16), block_diag(jnp.where(pm, low[m], 0.0))) for m in pr]
        tinv = [tinv[m] - _dot(xc[m].astype(BF16), block_diag(tinv[m])) for m in pr]
    tb = [tinv[m].astype(BF16) for m in pr]
    eg = [jnp.exp(gcol[h]) for h in hr]
    glast = [jnp.where(fwd, gcol[h][c - 1:c, :], gcol[h][0:1, :]) for h in hr]
    s_old = [s_ref[h] for h in hr]
    sb = [s_old[h].astype(BF16) for h in hr]
    ks_qs = [_dot(jnp.concatenate([(eg[h] * kf[h // 2]).astype(BF16),
                                   (qf[h // 2] * (scale * eg[h])).astype(BF16)], axis=0), sb[h]) for h in hr]
    rhs = [(beta[h] * (v_ref[0, :, lanes(h)].astype(F32) - ks_qs[h][0:c])).astype(BF16) for h in hr]
    vnb = [_dot(tb[m], block_diag_wide(rhs[2 * m], rhs[2 * m + 1])).astype(BF16) for m in pr]
    for m in pr:
        intra = _dot((qk[m] * gam[m]).astype(BF16),
                     block_diag_wide(vnb[m][:, 0:HEAD_DIM], vnb[m][:, HEAD_DIM:2 * HEAD_DIM]))
        inter = jnp.concatenate([ks_qs[2 * m][c:2 * c], ks_qs[2 * m + 1][c:2 * c]], axis=1)
        o_ref[0, 0, :, 2 * m * HEAD_DIM:(2 * m + 2) * HEAD_DIM] = (inter + intra).astype(BF16)
    for h in hr:
        k_dec = (kf[h // 2] * jnp.exp(glast[h] - gcol[h])).astype(BF16)
        vn = vnb[h // 2][:, (h % 2) * HEAD_DIM:(h % 2 + 1) * HEAD_DIM]
        s_ref[h] = s_old[h] * jnp.exp(glast[h]) + _dot_tn(k_dec, vn)


def _dn_chunks(qkv, beta_g, gcc_g, gcr_g, ctx_len, qk_w, hv):
    b, t, _ = qkv.shape
    g = beta_g.shape[-1]
    assert hv % g == 0 and qk_w % (g // 2 * HEAD_DIM) == 0
    nc = t // DN_CHUNK
    nctx = ctx_len // DN_CHUNK
    qw = g // 2 * HEAD_DIM
    vw = g * HEAD_DIM
    k_off = qk_w // qw
    v_off = 2 * qk_w // vw

    def chunk(d, p):
        back = jnp.where(p < nctx, nctx - 1 - p, nc + nctx - 1 - p)
        return jnp.where(d == 0, p, back)

    return pl.pallas_call(
        functools.partial(_dn_chunk_kernel, heads=g, scale=HEAD_DIM ** -0.5),
        grid=(b, hv // g, 2, nc),
        in_specs=[pl.BlockSpec((1, DN_CHUNK, qw), lambda bi, gi, d, p: (bi, chunk(d, p), gi)),
                  pl.BlockSpec((1, DN_CHUNK, qw), lambda bi, gi, d, p: (bi, chunk(d, p), k_off + gi)),
                  pl.BlockSpec((1, DN_CHUNK, vw), lambda bi, gi, d, p: (bi, chunk(d, p), v_off + gi)),
                  pl.BlockSpec((1, 1, 1, DN_CHUNK, g), lambda bi, gi, d, p: (bi, d, gi, chunk(d, p), 0)),
                  pl.BlockSpec((1, 1, 1, DN_CHUNK, g), lambda bi, gi, d, p: (bi, d, gi, chunk(d, p), 0)),
                  pl.BlockSpec((1, 1, 1, 1, g // 2, 2 * DN_CHUNK),
                               lambda bi, gi, d, p: (bi, d, gi, chunk(d, p), 0, 0))],
        out_specs=pl.BlockSpec((1, 1, DN_CHUNK, vw), lambda bi, gi, d, p: (bi, d, chunk(d, p), gi)),
        out_shape=jax.ShapeDtypeStruct((b, 2, t, hv * HEAD_DIM), BF16),
        scratch_shapes=[pltpu.VMEM((g, HEAD_DIM, HEAD_DIM), F32)],
        compiler_params=_params(("parallel", "parallel", "arbitrary", "arbitrary")),
        name="dn_chunks",
    )(qkv, qkv, qkv, beta_g, gcc_g, gcr_g)


def _dn_finish_kernel(o_ref, z_ref, w_ref, y_ref):
    cw = y_ref.shape[2]
    for h in range(cw // HEAD_DIM):
        hl = slice(h * HEAD_DIM, (h + 1) * HEAD_DIM)
        o = o_ref[0, 0, :, hl].astype(F32) + o_ref[0, 1, :, hl].astype(F32)
        ms = jnp.mean(o * o, axis=-1, keepdims=True)
        y = o * lax.rsqrt(ms + RMS_EPS) * w_ref[...]
        y_ref[0, :, hl] = (y * _silu(z_ref[0, :, hl].astype(F32))).astype(BF16)


def _dn_finish(o, proj, onorm_w, z_off):
    b, _, t, vw = o.shape
    cw = 1024 if vw % 1024 == 0 else vw
    assert z_off % cw == 0
    tr = math.gcd(t, 256)
    return pl.pallas_call(
        _dn_finish_kernel,
        grid=(b, t // tr, vw // cw),
        in_specs=[pl.BlockSpec((1, 2, tr, cw), lambda bi, ri, ci: (bi, 0, ri, ci)),
                  pl.BlockSpec((1, tr, cw), lambda bi, ri, ci: (bi, ri, z_off // cw + ci)),
                  pl.BlockSpec((1, HEAD_DIM), lambda bi, ri, ci: (0, 0))],
        out_specs=pl.BlockSpec((1, tr, cw), lambda bi, ri, ci: (bi, ri, ci)),
        out_shape=jax.ShapeDtypeStruct((b, t, vw), BF16),
        compiler_params=_params(("parallel", "parallel", "parallel")),
        name="dn_finish",
    )(o, proj, onorm_w.reshape(1, HEAD_DIM))


def _deltanet(proj, ba, conv_w, a_log, dt_bias, onorm_w, ctx_len, d_model):
    hk = d_model // HEAD_DIM
    hv = 2 * hk
    qk_w = hk * HEAD_DIM
    conv_ch = 2 * qk_w + hv * HEAD_DIM
    b, t, _ = proj.shape
    g = min(DN_HEADS_PER_STEP, hv)
    qkv = _dn_conv(proj, jnp.transpose(conv_w), ctx_len, conv_ch, 2 * hk)
    zeros = jnp.zeros((hv,), F32)
    alog_row = jnp.concatenate([zeros, a_log[0], zeros, a_log[1]]).reshape(1, 4 * hv)
    dtb_row = jnp.concatenate([zeros, dt_bias[0], zeros, dt_bias[1]]).reshape(1, 4 * hv)
    beta, gc = _dn_gates(ba, alog_row, dtb_row, hv)
    nc = t // DN_CHUNK
    beta_g = beta.reshape(b, 2, t, hv // g, g).transpose(0, 1, 3, 2, 4)
    gcc_g = gc.reshape(b, 2, t, hv // g, g).transpose(0, 1, 3, 2, 4)
    gcr_g = (gc.reshape(b, 2, nc, DN_CHUNK, hv // g, g).transpose(0, 1, 4, 2, 5, 3)
             .reshape(b, 2, hv // g, nc, g // 2, 2 * DN_CHUNK))
    o = _dn_chunks(qkv, beta_g, gcc_g, gcr_g, ctx_len, qk_w, hv)
    return _dn_finish(o, proj, onorm_w, conv_ch)


def _rope(x, cos, sin):
    quarter = HEAD_DIM // 4
    lane = lax.broadcasted_iota(jnp.int32, x.shape, 1)
    up = pltpu.roll(x, HEAD_DIM - quarter, 1)
    down = pltpu.roll(x, quarter, 1)
    rot = jnp.where((lane % (2 * quarter)) < quarter, -up, down)
    return x * cos + rot * sin


def _da_kernel(q_ref, k_ref, v_ref, g_ref, cos_ref, sin_ref, lam_ref, sw_ref, o_ref,
               krot_ref, q_scr, m_ref, l_ref, acc_ref, *, ctx_len, lambda_init, scale, tq, ck):
    qi = pl.program_id(2)
    t = k_ref.shape[1]
    maps = range(2)
    lanes = lambda j: slice(j * HEAD_DIM, (j + 1) * HEAD_DIM)
    qscale = scale * math.log2(math.e)
    lam = lam_ref[...]
    lam_full = (jnp.exp(jnp.sum(lam[0:1] * lam[1:2], axis=-1, keepdims=True))
                - jnp.exp(jnp.sum(lam[2:3] * lam[3:4], axis=-1, keepdims=True)) + lambda_init)

    def finish(o, gate):
        ms = jnp.mean(o * o, axis=-1, keepdims=True)
        y = o * lax.rsqrt(ms + DA_SUBLN_EPS) * sw_ref[...] * (1.0 - lambda_init)
        return (y * _silu(gate.astype(F32))).astype(BF16)

    @pl.when(qi == 0)
    def _():
        for j in maps:
            kk = k_ref[0, :, lanes(j)].astype(F32)
            krot_ref[:, lanes(j)] = _rope(kk, cos_ref[...], sin_ref[...]).astype(BF16)
        outs = []
        for j in maps:
            q = _rope(q_ref[0, 0:ctx_len, lanes(j)].astype(F32), cos_ref[0:ctx_len, :], sin_ref[0:ctx_len, :])
            s = _dot_nt((q * qscale).astype(BF16), krot_ref[0:ctx_len, lanes(j)])
            pr = jnp.exp2(s - jnp.max(s, axis=-1, keepdims=True))
            outs.append(_dot(pr.astype(BF16), v_ref[0, 0:ctx_len, :]) / jnp.sum(pr, axis=-1, keepdims=True))
        o_ref[0, 0:ctx_len, :] = finish(outs[0] - lam_full * outs[1], g_ref[0, 0:ctx_len, :])

    @pl.when(qi > 0)
    def _():
        r0 = pl.multiple_of(ctx_len + (qi - 1) * tq, math.gcd(ctx_len, tq))
        rows = pl.ds(r0, tq)
        for j in maps:
            q = _rope(q_ref[0, rows, lanes(j)].astype(F32), cos_ref[rows, :], sin_ref[rows, :])
            q_scr[j] = (q * qscale).astype(BF16)

        stat = (tq, HEAD_DIM)

        def widen(x, width):
            if width % HEAD_DIM == 0:
                return pltpu.repeat(x, width // HEAD_DIM, 1)
            return jnp.broadcast_to(x[:, 0:1], (tq, width))

        for j in maps:
            s = _dot_nt(q_scr[j], krot_ref[0:ctx_len, lanes(j)])
            m = jnp.broadcast_to(jnp.max(s, axis=-1, keepdims=True), stat)
            pr = jnp.exp2(s - widen(m, ctx_len))
            m_ref[j] = m
            l_ref[j] = jnp.broadcast_to(jnp.sum(pr, axis=-1, keepdims=True), stat)
            acc_ref[j] = _dot(pr.astype(BF16), v_ref[0, 0:ctx_len, :])

        def body(c, carry):
            k0 = pl.multiple_of(ctx_len + c * ck, math.gcd(ctx_len, ck))
            s = [_dot_nt(q_scr[j], krot_ref[pl.ds(k0, ck), lanes(j)]) for j in maps]
            m_old = [m_ref[j] for j in maps]
            m_new = [jnp.maximum(m_old[j], jnp.broadcast_to(jnp.max(s[j], axis=-1, keepdims=True), stat))
                     for j in maps]
            alpha = [jnp.exp2(m_old[j] - m_new[j]) for j in maps]
            pr = [jnp.exp2(s[j] - widen(m_new[j], ck)) for j in maps]
            v = v_ref[0, pl.ds(k0, ck), :]
            for j in maps:
                m_ref[j] = m_new[j]
                l_ref[j] = alpha[j] * l_ref[j] + jnp.broadcast_to(jnp.sum(pr[j], axis=-1, keepdims=True), stat)
                acc_ref[j] = widen(alpha[j], 2 * HEAD_DIM) * acc_ref[j] + _dot(pr[j].astype(BF16), v)
            return carry

        lax.fori_loop(0, (t - ctx_len) // ck, body, 0)
        o = (acc_ref[0] / widen(l_ref[0], 2 * HEAD_DIM)
             - lam_full * (acc_ref[1] / widen(l_ref[1], 2 * HEAD_DIM)))
        o_ref[0, rows, :] = finish(o, g_ref[0, rows, :])


def _rope_tables(ctx_len, n):
    rows = n // GRID_W
    half = HEAD_DIM // 2
    inv = ROPE_THETA ** (-jnp.arange(0, half, 2, dtype=F32) / half)
    ang_r = jnp.arange(rows, dtype=F32)[:, None] * inv
    ang_c = jnp.arange(GRID_W, dtype=F32)[:, None] * inv

    def table(fn, ctx_value):
        fr = jnp.broadcast_to(fn(ang_r)[:, None, :], (rows, GRID_W, half // 2))
        fc = jnp.broadcast_to(fn(ang_c)[None, :, :], (rows, GRID_W, half // 2))
        lat = jnp.concatenate([fr, fr, fc, fc], axis=-1).reshape(n, HEAD_DIM)
        return jnp.concatenate([jnp.full((ctx_len, HEAD_DIM), ctx_value, F32), lat], axis=0)

    return table(jnp.cos, 1.0), table(jnp.sin, 0.0)


def _diff_attn(proj, lam, subln_w, ctx_len, d_model, layer_idx):
    b, t, _ = proj.shape
    hw = 2 * HEAD_DIM
    heads = d_model // hw
    n = t - ctx_len
    tq = min(DA_QUERY_TILE, n)
    ck = min(DA_KEY_CHUNK, n)
    assert n % tq == 0 and n % ck == 0
    lambda_init = 0.8 - 0.6 * math.exp(-0.3 * layer_idx)
    cos, sin = _rope_tables(ctx_len, n)
    full = lambda bi, hi, qi: (0, 0)
    once = pl.Buffered(1)
    col = lambda off: (lambda bi, hi, qi: (bi, 0, off * heads + hi))
    return pl.pallas_call(
        functools.partial(_da_kernel, ctx_len=ctx_len, lambda_init=lambda_init, scale=HEAD_DIM ** -0.5,
                          tq=tq, ck=ck),
        grid=(b, heads, 1 + n // tq),
        in_specs=[pl.BlockSpec((1, t, hw), col(0)),
                  pl.BlockSpec((1, t, hw), col(1)),
                  pl.BlockSpec((1, t, hw), col(2)),
                  pl.BlockSpec((1, t, hw), col(3)),
                  pl.BlockSpec((t, HEAD_DIM), full, pipeline_mode=once),
                  pl.BlockSpec((t, HEAD_DIM), full, pipeline_mode=once),
                  pl.BlockSpec((4, HEAD_DIM), full),
                  pl.BlockSpec((1, hw), full)],
        out_specs=pl.BlockSpec((1, t, hw), col(0)),
        out_shape=jax.ShapeDtypeStruct((b, t, heads * hw), BF16),
        scratch_shapes=[pltpu.VMEM((t, hw), BF16),
                        pltpu.VMEM((2, tq, HEAD_DIM), BF16),
                        pltpu.VMEM((2, tq, HEAD_DIM), F32),
                        pltpu.VMEM((2, tq, HEAD_DIM), F32),
                        pltpu.VMEM((2, tq, hw), F32)],
        compiler_params=_params(("parallel", "parallel", "arbitrary")),
        name="diff_attn",
    )(proj, proj, proj, proj, cos, sin, lam, subln_w.reshape(1, hw))


def _na_kernel(q_ref, k_ref, v_ref, g_ref, bias_ref, o_ref, *, ctx_len, rows, scale):
    wk = NA_WR * GRID_W
    kc = k_ref[0, 0:ctx_len, :]
    vc = v_ref[0, 0:ctx_len, :]

    s = _dot_nt(q_ref[0, 0:ctx_len, :], kc) * scale
    m = jnp.max(s, axis=-1, keepdims=True)
    pr = jnp.exp(s - m)
    o = _dot(pr.astype(BF16), vc) / jnp.sum(pr, axis=-1, keepdims=True)
    o_ref[0, 0:ctx_len, :] = (o * _silu(g_ref[0, 0:ctx_len, :].astype(F32))).astype(BF16)

    def body(it, carry):
        nr = range(NA_ROWS_PER_ITER)
        r = [it * NA_ROWS_PER_ITER + i for i in nr]
        rs = [jnp.clip(r[i] - NA_WR // 2, 0, rows - NA_WR) for i in nr]
        qs = [pl.multiple_of(ctx_len + r[i] * GRID_W, GRID_W) for i in nr]
        ks = [pl.multiple_of(ctx_len + rs[i] * GRID_W, GRID_W) for i in nr]
        q = [q_ref[0, pl.ds(qs[i], GRID_W), :] for i in nr]
        s_win = [_dot_nt(q[i], k_ref[0, pl.ds(ks[i], wk), :]) * scale + bias_ref[0, rs[i] - r[i] + NA_WR - 1]
                 for i in nr]
        s_ctx = [_dot_nt(q[i], kc) * scale for i in nr]
        mx = [jnp.maximum(jnp.max(s_win[i], axis=-1, keepdims=True), jnp.max(s_ctx[i], axis=-1, keepdims=True))
              for i in nr]
        p_win = [jnp.exp(s_win[i] - mx[i]) for i in nr]
        p_ctx = [jnp.exp(s_ctx[i] - mx[i]) for i in nr]
        den = [jnp.sum(p_win[i], axis=-1, keepdims=True) + jnp.sum(p_ctx[i], axis=-1, keepdims=True) for i in nr]
        out = [(_dot(p_win[i].astype(BF16), v_ref[0, pl.ds(ks[i], wk), :]) + _dot(p_ctx[i].astype(BF16), vc)) / den[i]
               for i in nr]
        for i in nr:
            gate = g_ref[0, pl.ds(qs[i], GRID_W), :].astype(F32)
            o_ref[0, pl.ds(qs[i], GRID_W), :] = (out[i] * _silu(gate)).astype(BF16)
        return carry

    lax.fori_loop(0, rows // NA_ROWS_PER_ITER, body, 0)


def _na_bias_table(rpb):
    heads = rpb.shape[0]
    cols = jnp.arange(GRID_W)
    start = jnp.clip(cols - NA_WC // 2, 0, GRID_W - NA_WC)
    inside = (cols[None, :] >= start[:, None]) & (cols[None, :] < start[:, None] + NA_WC)
    dc = cols[None, :] - cols[:, None] + NA_WC - 1
    rpb = rpb.astype(F32)
    wide = jnp.full((heads, 2 * NA_WR - 1, GRID_W, GRID_W), NEG_BIG, F32)
    for off in range(2 * NA_WC - 1):
        wide = jnp.where((inside & (dc == off))[None, None], rpb[:, :, off, None, None], wide)
    tab = jnp.stack([wide[:, o:o + NA_WR] for o in range(NA_WR)], axis=1)
    return tab.transpose(0, 1, 3, 2, 4).reshape(heads, NA_WR, GRID_W, NA_WR * GRID_W)


def _neighbourhood(proj, rpb, ctx_len, d_model):
    b, t, _ = proj.shape
    heads = d_model // HEAD_DIM
    rows = (t - ctx_len) // GRID_W
    assert rows >= NA_WR
    wk = NA_WR * GRID_W
    bias = _na_bias_table(rpb)
    col = lambda off: (lambda bi, hi: (bi, 0, off * heads + hi))
    return pl.pallas_call(
        functools.partial(_na_kernel, ctx_len=ctx_len, rows=rows, scale=HEAD_DIM ** -0.5),
        grid=(b, heads),
        in_specs=[pl.BlockSpec((1, t, HEAD_DIM), col(0)),
                  pl.BlockSpec((1, t, HEAD_DIM), col(1)),
                  pl.BlockSpec((1, t, HEAD_DIM), col(2)),
                  pl.BlockSpec((1, t, HEAD_DIM), col(3)),
                  pl.BlockSpec((1, NA_WR, GRID_W, wk), lambda bi, hi: (hi, 0, 0, 0))],
        out_specs=pl.BlockSpec((1, t, HEAD_DIM), col(0)),
        out_shape=jax.ShapeDtypeStruct((b, t, heads * HEAD_DIM), BF16),
        compiler_params=_params(("parallel", "parallel")),
        name="neighbourhood",
    )(proj, proj, proj, proj, bias)


def kernel(x, c, ctx, c_ctx, norm_w, ada_w, ada_b, dn_w_in, dn_conv_w, dn_a_log, dn_dt_bias, dn_onorm_w,
           dn_w_out, da_w_in, da_lambda, da_subln_w, da_w_out, na_w_in, na_rpb, na_w_out, final_norm_w):
    b, n, d = x.shape
    ctx_len = ctx.shape[1]
    t = ctx_len + n
    depth = norm_w.shape[0]
    assert b + 1 <= MOD_ROWS and n % GRID_W == 0 and ctx_len % DN_CHUNK == 0
    tm = t // 4
    assert t % 4 == 0 and tm % 16 == 0

    xa = jnp.concatenate([ctx, x], axis=1)
    cvec = jnp.concatenate([c, c_ctx[None, :], jnp.zeros((MOD_ROWS - b - 1, d), F32)], axis=0)
    mods = _modulation(cvec, ada_w, ada_b)

    conv_ch = 4 * d
    for i in range(depth):
        kind, j = i % N_MIXERS, i // N_MIXERS
        sh, sc, gt = mods[i, :, 0:d], mods[i, :, d:2 * d], mods[i, :, 2 * d:3 * d]
        lat = lambda m: m[0:b, None, :]
        cx = lambda m: m[b:b + 1, :]
        if kind == 0:
            w_in = dn_w_in[j]
            proj, ba = _norm_proj(xa, norm_w[i], lat(sc), lat(sh), cx(sc), cx(sh),
                                  w_in[:, 0:conv_ch + 2 * d].astype(BF16), w_in[:, conv_ch + 2 * d:].astype(BF16),
                                  ctx_len, tm)
            y = _deltanet(proj, ba, dn_conv_w[j], dn_a_log[j], dn_dt_bias[j], dn_onorm_w[j], ctx_len, d)
            w_out = dn_w_out[j]
        elif kind == 1:
            proj, _ = _norm_proj(xa, norm_w[i], lat(sc), lat(sh), cx(sc), cx(sh),
                                 da_w_in[j].astype(BF16), None, ctx_len, tm)
            y = _diff_attn(proj, da_lambda[j], da_subln_w[j], ctx_len, d, i)
            w_out = da_w_out[j]
        else:
            proj, _ = _norm_proj(xa, norm_w[i], lat(sc), lat(sh), cx(sc), cx(sh),
                                 na_w_in[j].astype(BF16), None, ctx_len, tm)
            y = _neighbourhood(proj, na_rpb[j], ctx_len, d)
            w_out = na_w_out[j]
        xa = _out_proj(y, w_out.astype(BF16), xa, lat(gt), cx(gt), ctx_len, tm)
    return _final_norm(xa, final_norm_w, ctx_len)
```

```python
import functools
import math

import jax
import jax.numpy as jnp
from jax import lax
from jax.experimental import pallas as pl
from jax.experimental.pallas import tpu as pltpu

F32 = jnp.float32
BF16 = jnp.bfloat16

GRID_W = 64
N_MIXERS = 3
RMS_EPS = 1e-6
HEAD_DIM = 128
DN_CONV_K = 5
DN_CHUNK = 64
DN_HEADS_PER_STEP = 32
DA_SUBLN_EPS = 1e-5
DA_QUERY_TILE = 1024
DA_KEY_CHUNK = 1024
ROPE_THETA = 10000.0
NA_WR = 8
NA_WC = 16
NA_ROWS_PER_ITER = 8
NEG_BIG = -1e30
PROJ_TN = 512
MOD_ROWS = 8
V7X_VMEM_LIMIT = 56 * 1024 * 1024


def _sigmoid(x):
    return 1.0 / (1.0 + jnp.exp(-x))


def _silu(x):
    return x * _sigmoid(x)


def _dot(a, b):
    return jnp.dot(a, b, preferred_element_type=F32)


def _dot_nt(a, b):
    return lax.dot_general(a, b, (((1,), (1,)), ((), ())), preferred_element_type=F32)


def _dot_tn(a, b):
    return lax.dot_general(a, b, (((0,), (0,)), ((), ())), preferred_element_type=F32)


def _params(semantics, vmem=V7X_VMEM_LIMIT):
    return pltpu.CompilerParams(dimension_semantics=semantics, vmem_limit_bytes=vmem)


def _mod_kernel(c_ref, w_ref, b_ref, o_ref):
    s = _silu(c_ref[...])
    o_ref[0] = jnp.dot(s, w_ref[0], preferred_element_type=F32,
                       precision=lax.Precision.HIGHEST) + b_ref[0]


def _modulation(cvec, ada_w, ada_b):
    depth, d, p = ada_w.shape
    tn = 1024 if p % 1024 == 0 else p
    return pl.pallas_call(
        _mod_kernel,
        grid=(depth, p // tn),
        in_specs=[pl.BlockSpec((MOD_ROWS, d), lambda i, n: (0, 0)),
                  pl.BlockSpec((1, d, tn), lambda i, n: (i, 0, n)),
                  pl.BlockSpec((1, 1, tn), lambda i, n: (i, 0, n))],
        out_specs=pl.BlockSpec((1, MOD_ROWS, tn), lambda i, n: (i, 0, n)),
        out_shape=jax.ShapeDtypeStruct((depth, MOD_ROWS, p), F32),
        compiler_params=_params(("parallel", "parallel")),
        name="adaln_mod",
    )(cvec, ada_w, ada_b.reshape(depth, 1, p))


def _norm_proj_kernel(*refs, ctx_len, tm, has_extra):
    if has_extra:
        x_ref, nw_ref, scl_ref, shl_ref, scc_ref, shc_ref, w_ref, wx_ref, o_ref, ox_ref, h_ref = refs
    else:
        x_ref, nw_ref, scl_ref, shl_ref, scc_ref, shc_ref, w_ref, o_ref, h_ref = refs
    mi = pl.program_id(1)
    ni = pl.program_id(2)

    @pl.when(ni == 0)
    def _():
        x = x_ref[0]
        ms = jnp.mean(x * x, axis=-1, keepdims=True)
        y = x * lax.rsqrt(ms + RMS_EPS) * nw_ref[...]
        row = lax.broadcasted_iota(jnp.int32, (tm, 1), 0) + mi * tm
        is_ctx = row < ctx_len
        sc = jnp.where(is_ctx, scc_ref[...], scl_ref[0])
        sh = jnp.where(is_ctx, shc_ref[...], shl_ref[0])
        h = (y * (1.0 + sc) + sh).astype(BF16)
        h_ref[...] = h
        if has_extra:
            ox_ref[0] = _dot(h, wx_ref[...])

    o_ref[0] = _dot(h_ref[...], w_ref[...]).astype(BF16)


def _norm_proj(xa, nw, scl, shl, scc, shc, w, wx, ctx_len, tm):
    b, t, d = xa.shape
    p = w.shape[1]
    tn = PROJ_TN if p % PROJ_TN == 0 else PROJ_TN // 2
    assert t % tm == 0 and p % tn == 0
    has_extra = wx is not None
    vec_l = pl.BlockSpec((1, 1, d), lambda bi, mi, ni: (bi, 0, 0))
    vec_c = pl.BlockSpec((1, d), lambda bi, mi, ni: (0, 0))
    in_specs = [pl.BlockSpec((1, tm, d), lambda bi, mi, ni: (bi, mi, 0)),
                vec_c, vec_l, vec_l, vec_c, vec_c,
                pl.BlockSpec((d, tn), lambda bi, mi, ni: (0, ni))]
    out_specs = [pl.BlockSpec((1, tm, tn), lambda bi, mi, ni: (bi, mi, ni))]
    out_shape = [jax.ShapeDtypeStruct((b, t, p), BF16)]
    args = [xa, nw.reshape(1, d), scl, shl, scc, shc, w]
    if has_extra:
        px = wx.shape[1]
        in_specs.append(pl.BlockSpec((d, px), lambda bi, mi, ni: (0, 0)))
        out_specs.append(pl.BlockSpec((1, tm, px), lambda bi, mi, ni: (bi, mi, 0)))
        out_shape.append(jax.ShapeDtypeStruct((b, t, px), F32))
        args.append(wx)
    outs = pl.pallas_call(
        functools.partial(_norm_proj_kernel, ctx_len=ctx_len, tm=tm, has_extra=has_extra),
        grid=(b, t // tm, p // tn),
        in_specs=in_specs,
        out_specs=out_specs,
        out_shape=out_shape,
        scratch_shapes=[pltpu.VMEM((tm, d), BF16)],
        compiler_params=_params(("parallel", "parallel", "arbitrary")),
        name="norm_proj",
    )(*args)
    return (outs[0], outs[1]) if has_extra else (outs[0], None)


def _out_proj_kernel(y_ref, w_ref, x_ref, gtl_ref, gtc_ref, o_ref, *, ctx_len, tm):
    mi = pl.program_id(1)
    acc = _dot(y_ref[0], w_ref[...])
    row = lax.broadcasted_iota(jnp.int32, (tm, 1), 0) + mi * tm
    gt = jnp.where(row < ctx_len, gtc_ref[...], gtl_ref[0])
    o_ref[0] = x_ref[0] + gt * acc


def _out_proj(y, w, xa, gtl, gtc, ctx_len, tm):
    b, t, d = xa.shape
    kd = y.shape[2]
    tn = PROJ_TN if d % PROJ_TN == 0 else PROJ_TN // 2
    while 2 * (tm * kd * 2 + kd * tn * 2 + 2 * tm * tn * 4) > V7X_VMEM_LIMIT * 3 // 4:
        tn //= 2
    assert d % tn == 0 and t % tm == 0
    return pl.pallas_call(
        functools.partial(_out_proj_kernel, ctx_len=ctx_len, tm=tm),
        grid=(b, t // tm, d // tn),
        in_specs=[pl.BlockSpec((1, tm, kd), lambda bi, mi, ni: (bi, mi, 0)),
                  pl.BlockSpec((kd, tn), lambda bi, mi, ni: (0, ni)),
                  pl.BlockSpec((1, tm, tn), lambda bi, mi, ni: (bi, mi, ni)),
                  pl.BlockSpec((1, 1, tn), lambda bi, mi, ni: (bi, 0, ni)),
                  pl.BlockSpec((1, tn), lambda bi, mi, ni: (0, ni))],
        out_specs=pl.BlockSpec((1, tm, tn), lambda bi, mi, ni: (bi, mi, ni)),
        out_shape=jax.ShapeDtypeStruct((b, t, d), F32),
        compiler_params=_params(("parallel", "parallel", "arbitrary")),
        name="out_proj",
    )(y, w, xa, gtl, gtc)


def _final_norm_kernel(x_ref, w_ref, o_ref):
    x = x_ref[0]
    ms = jnp.mean(x * x, axis=-1, keepdims=True)
    o_ref[0] = x * lax.rsqrt(ms + RMS_EPS) * w_ref[...]


def _final_norm(xa, w, ctx_len):
    b, t, d = xa.shape
    n = t - ctx_len
    tr = math.gcd(ctx_len, 256)
    assert n % tr == 0
    off = ctx_len // tr
    return pl.pallas_call(
        _final_norm_kernel,
        grid=(b, n // tr),
        in_specs=[pl.BlockSpec((1, tr, d), lambda bi, ri: (bi, ri + off, 0)),
                  pl.BlockSpec((1, d), lambda bi, ri: (0, 0))],
        out_specs=pl.BlockSpec((1, tr, d), lambda bi, ri: (bi, ri, 0)),
        out_shape=jax.ShapeDtypeStruct((b, n, d), F32),
        compiler_params=_params(("parallel", "parallel")),
        name="final_norm",
    )(xa, w.reshape(1, d))


def _dn_conv_kernel(x_ref, w_ref, o_ref, *, ctx_len, n_qk_blocks):
    blk = pl.program_id(1)
    x = x_ref[0].astype(F32)
    t = x.shape[0]
    w = w_ref[...]
    row = lax.broadcasted_iota(jnp.int32, x.shape, 0)
    seg = row >= ctx_len
    half = DN_CONV_K // 2
    acc = x * w[half:half + 1, :]
    for tap in range(DN_CONV_K):
        dlt = tap - half
        if dlt == 0:
            continue
        shifted = pltpu.roll(x, (-dlt) % t, 0)
        src = row + dlt
        ok = (src >= 0) & (src < t) & ((src >= ctx_len) == seg)
        acc = acc + jnp.where(ok, shifted, 0.0) * w[tap:tap + 1, :]
    y = _silu(acc)
    inv = lax.rsqrt(jnp.sum(y * y, axis=-1, keepdims=True) + 1e-6)
    y = y * jnp.where(blk < n_qk_blocks, inv, 1.0)
    o_ref[0] = y.astype(BF16)


def _dn_conv(proj, conv_w_t, ctx_len, conv_ch, n_qk_blocks):
    b, t, _ = proj.shape
    return pl.pallas_call(
        functools.partial(_dn_conv_kernel, ctx_len=ctx_len, n_qk_blocks=n_qk_blocks),
        grid=(b, conv_ch // HEAD_DIM),
        in_specs=[pl.BlockSpec((1, t, HEAD_DIM), lambda bi, ci: (bi, 0, ci)),
                  pl.BlockSpec((DN_CONV_K, HEAD_DIM), lambda bi, ci: (0, ci))],
        out_specs=pl.BlockSpec((1, t, HEAD_DIM), lambda bi, ci: (bi, 0, ci)),
        out_shape=jax.ShapeDtypeStruct((b, t, conv_ch), BF16),
        compiler_params=_params(("parallel", "parallel")),
        name="dn_conv",
    )(proj, conv_w_t)


def _exact_f32_dot(tri, g):
    hi = g.astype(BF16)
    r1 = g - hi.astype(F32)
    mid = r1.astype(BF16)
    lo = (r1 - mid.astype(F32)).astype(BF16)
    return _dot(tri, hi) + _dot(tri, mid) + _dot(tri, lo)


def _dn_gate_kernel(ba_ref, alog_ref, dtb_ref, beta_ref, gc_ref, *, hv):
    ii = lax.broadcasted_iota(jnp.int32, (DN_CHUNK, DN_CHUNK), 0)
    jj = lax.broadcasted_iota(jnp.int32, (DN_CHUNK, DN_CHUNK), 1)
    lower = jnp.where(ii >= jj, 1.0, 0.0).astype(BF16)
    upper = jnp.where(ii <= jj, 1.0, 0.0).astype(BF16)
    ba = ba_ref[0]
    z = ba + dtb_ref[...]
    softplus = jnp.maximum(z, 0.0) + jnp.log1p(jnp.exp(-jnp.abs(z)))
    g = -jnp.exp(alog_ref[...]) * softplus
    beta = _sigmoid(ba)
    beta_ref[0, 0] = beta[:, 0:hv]
    beta_ref[0, 1] = beta[:, 2 * hv:3 * hv]
    for ci in range(ba.shape[0] // DN_CHUNK):
        rows = slice(ci * DN_CHUNK, (ci + 1) * DN_CHUNK)
        gc_ref[0, 0, rows, :] = _exact_f32_dot(lower, g[rows])[:, hv:2 * hv]
        gc_ref[0, 1, rows, :] = _exact_f32_dot(upper, g[rows])[:, 3 * hv:4 * hv]


def _dn_gates(ba, alog_row, dtb_row, hv):
    b, t, pw = ba.shape
    out = jax.ShapeDtypeStruct((b, 2, t, hv), F32)
    rows = next(r * DN_CHUNK for r in (4, 2, 1) if t % (r * DN_CHUNK) == 0)
    ospec = pl.BlockSpec((1, 2, rows, hv), lambda bi, ci: (bi, 0, ci, 0))
    return pl.pallas_call(
        functools.partial(_dn_gate_kernel, hv=hv),
        grid=(b, t // rows),
        in_specs=[pl.BlockSpec((1, rows, pw), lambda bi, ci: (bi, ci, 0)),
                  pl.BlockSpec((1, pw), lambda bi, ci: (0, 0)),
                  pl.BlockSpec((1, pw), lambda bi, ci: (0, 0))],
        out_specs=[ospec, ospec],
        out_shape=[out, out],
        compiler_params=_params(("parallel", "parallel")),
        name="dn_gates",
    )(ba, alog_row, dtb_row)


def _dn_chunk_kernel(q_ref, k_ref, v_ref, beta_ref, gcc_ref, gcr_ref, o_ref, s_ref, *, heads, scale):
    d = pl.program_id(2)
    p = pl.program_id(3)

    @pl.when(p == 0)
    def _():
        s_ref[...] = jnp.zeros_like(s_ref)

    c = DN_CHUNK
    ii = lax.broadcasted_iota(jnp.int32, (c, 2 * c), 0)
    lane = lax.broadcasted_iota(jnp.int32, (c, 2 * c), 1)
    jj = lane % c
    left = lane < c
    order = (ii - jj) * (1 - 2 * d)
    incl = order >= 0
    strict = order > 0
    eye = jnp.where(ii == jj, 1.0, 0.0)
    pair_masks = []
    sz = 1
    while sz < c:
        pair_masks.append((ii // (2 * sz) == jj // (2 * sz)) & (ii // sz != jj // sz))
        sz *= 2
    fwd = d == 0
    hr = range(heads)
    pr = range(heads // 2)
    lanes = lambda i: slice(i * HEAD_DIM, (i + 1) * HEAD_DIM)
    zeros_bf = jnp.zeros((c, HEAD_DIM), BF16)

    def block_diag(x):
        return jnp.concatenate([jnp.where(left, x, 0.0), jnp.where(left, 0.0, x)], axis=0).astype(BF16)

    def block_diag_wide(a, b):
        return jnp.concatenate([jnp.concatenate([a, zeros_bf], axis=1),
                                jnp.concatenate([zeros_bf, b], axis=1)], axis=0)

    kb, kq, kk, qk = [], [], [], []
    for m in pr:
        q = q_ref[0, :, lanes(m)]
        k = k_ref[0, :, lanes(m)]
        kq.append(jnp.concatenate([k, q], axis=0))
        both = _dot_nt(kq[m], jnp.concatenate([k, k], axis=0))
        kk.append(both[0:c])
        qk.append(both[c:2 * c] * scale)
        kb.append(k)
    beta = [beta_ref[0, 0, 0, :, h:h + 1] for h in hr]
    gcol = [gcc_ref[0, 0, 0, :, h:h + 1] for h in hr]
    beta2 = [jnp.where(left, beta[2 * m], beta[2 * m + 1]) for m in pr]
    gcol2 = [jnp.where(left, gcol[2 * m], gcol[2 * m + 1]) for m in pr]
    grow2 = [gcr_ref[0, 0, 0, 0, m:m + 1, :] for m in pr]
    gam = [jnp.where(incl, jnp.exp(jnp.where(incl, gcol2[m] - grow2[m], 0.0)), 0.0) for m in pr]
    low = [jnp.where(strict, beta2[m] * kk[m] * gam[m], 0.0) for m in pr]
    tinv = [eye - jnp.where(pair_masks[0], low[m], 0.0) for m in pr]
    for pm in pair_masks[1:]:
        xc = [_dot(tinv[m].astype(BF16), block_diag(jnp.where(pm, low[m], 0.0))) for m in pr]
        tinv = [tinv[m] - _dot(xc[m].astype(BF16), block_diag(tinv[m])) for m in pr]
    tb = [tinv[m].astype(BF16) for m in pr]
    eg = [jnp.exp(gcol[h]) for h in hr]
    glast = [jnp.where(fwd, gcol[h][c - 1:c, :], gcol[h][0:1, :]) for h in hr]
    s_old = [s_ref[h] for h in hr]
    half = lambda h: lanes(h % 2)
    kqs = [_dot(kq[m], jnp.concatenate([s_old[2 * m].astype(BF16), s_old[2 * m + 1].astype(BF16)], axis=1))
           for m in pr]
    rhs = [(beta[h] * (v_ref[0, :, lanes(h)].astype(F32) - eg[h] * kqs[h // 2][0:c, half(h)])).astype(BF16)
           for h in hr]
    v_new = [_dot(tb[m], block_diag_wide(rhs[2 * m], rhs[2 * m + 1])) for m in pr]
    vnb = [v_new[m].astype(BF16) for m in pr]
    for m in pr:
        intra = _dot((qk[m] * gam[m]).astype(BF16),
                     block_diag_wide(vnb[m][:, 0:HEAD_DIM], vnb[m][:, HEAD_DIM:2 * HEAD_DIM]))
        inter = jnp.concatenate([(scale * eg[h]) * kqs[m][c:2 * c, half(h)] for h in (2 * m, 2 * m + 1)], axis=1)
        o_ref[0, 0, :, 2 * m * HEAD_DIM:(2 * m + 2) * HEAD_DIM] = (inter + intra).astype(BF16)
    wv = [jnp.concatenate([jnp.exp(glast[h] - gcol[h]) * v_new[m][:, half(h)] for h in (2 * m, 2 * m + 1)],
                          axis=1).astype(BF16) for m in pr]
    upd = [_dot_tn(kb[m], wv[m]) for m in pr]
    for h in hr:
        s_ref[h] = s_old[h] * jnp.exp(glast[h]) + upd[h // 2][:, half(h)]


def _dn_chunks(qkv, beta_g, gcc_g, gcr_g, ctx_len, qk_w, hv):
    b, t, _ = qkv.shape
    g = beta_g.shape[-1]
    assert hv % g == 0 and qk_w % (g // 2 * HEAD_DIM) == 0
    nc = t // DN_CHUNK
    nctx = ctx_len // DN_CHUNK
    qw = g // 2 * HEAD_DIM
    vw = g * HEAD_DIM
    k_off = qk_w // qw
    v_off = 2 * qk_w // vw

    def chunk(d, p):
        back = jnp.where(p < nctx, nctx - 1 - p, nc + nctx - 1 - p)
        return jnp.where(d == 0, p, back)

    return pl.pallas_call(
        functools.partial(_dn_chunk_kernel, heads=g, scale=HEAD_DIM ** -0.5),
        grid=(b, hv // g, 2, nc),
        in_specs=[pl.BlockSpec((1, DN_CHUNK, qw), lambda bi, gi, d, p: (bi, chunk(d, p), gi)),
                  pl.BlockSpec((1, DN_CHUNK, qw), lambda bi, gi, d, p: (bi, chunk(d, p), k_off + gi)),
                  pl.BlockSpec((1, DN_CHUNK, vw), lambda bi, gi, d, p: (bi, chunk(d, p), v_off + gi)),
                  pl.BlockSpec((1, 1, 1, DN_CHUNK, g), lambda bi, gi, d, p: (bi, d, gi, chunk(d, p), 0)),
                  pl.BlockSpec((1, 1, 1, DN_CHUNK, g), lambda bi, gi, d, p: (bi, d, gi, chunk(d, p), 0)),
                  pl.BlockSpec((1, 1, 1, 1, g // 2, 2 * DN_CHUNK),
                               lambda bi, gi, d, p: (bi, d, gi, chunk(d, p), 0, 0))],
        out_specs=pl.BlockSpec((1, 1, DN_CHUNK, vw), lambda bi, gi, d, p: (bi, d, chunk(d, p), gi)),
        out_shape=jax.ShapeDtypeStruct((b, 2, t, hv * HEAD_DIM), BF16),
        scratch_shapes=[pltpu.VMEM((g, HEAD_DIM, HEAD_DIM), F32)],
        compiler_params=_params(("parallel", "parallel", "arbitrary", "arbitrary")),
        name="dn_chunks",
    )(qkv, qkv, qkv, beta_g, gcc_g, gcr_g)


def _dn_finish_kernel(o_ref, z_ref, w_ref, y_ref):
    cw = y_ref.shape[2]
    for h in range(cw // HEAD_DIM):
        hl = slice(h * HEAD_DIM, (h + 1) * HEAD_DIM)
        o = o_ref[0, 0, :, hl].astype(F32) + o_ref[0, 1, :, hl].astype(F32)
        ms = jnp.mean(o * o, axis=-1, keepdims=True)
        y = o * lax.rsqrt(ms + RMS_EPS) * w_ref[...]
        y_ref[0, :, hl] = (y * _silu(z_ref[0, :, hl].astype(F32))).astype(BF16)


def _dn_finish(o, proj, onorm_w, z_off):
    b, _, t, vw = o.shape
    cw = 1024 if vw % 1024 == 0 else vw
    assert z_off % cw == 0
    tr = math.gcd(t, 256)
    return pl.pallas_call(
        _dn_finish_kernel,
        grid=(b, t // tr, vw // cw),
        in_specs=[pl.BlockSpec((1, 2, tr, cw), lambda bi, ri, ci: (bi, 0, ri, ci)),
                  pl.BlockSpec((1, tr, cw), lambda bi, ri, ci: (bi, ri, z_off // cw + ci)),
                  pl.BlockSpec((1, HEAD_DIM), lambda bi, ri, ci: (0, 0))],
        out_specs=pl.BlockSpec((1, tr, cw), lambda bi, ri, ci: (bi, ri, ci)),
        out_shape=jax.ShapeDtypeStruct((b, t, vw), BF16),
        compiler_params=_params(("parallel", "parallel", "parallel")),
        name="dn_finish",
    )(o, proj, onorm_w.reshape(1, HEAD_DIM))


def _deltanet(proj, ba, conv_w, a_log, dt_bias, onorm_w, ctx_len, d_model):
    hk = d_model // HEAD_DIM
    hv = 2 * hk
    qk_w = hk * HEAD_DIM
    conv_ch = 2 * qk_w + hv * HEAD_DIM
    b, t, _ = proj.shape
    g = min(DN_HEADS_PER_STEP, hv)
    qkv = _dn_conv(proj, jnp.transpose(conv_w), ctx_len, conv_ch, 2 * hk)
    zeros = jnp.zeros((hv,), F32)
    alog_row = jnp.concatenate([zeros, a_log[0], zeros, a_log[1]]).reshape(1, 4 * hv)
    dtb_row = jnp.concatenate([zeros, dt_bias[0], zeros, dt_bias[1]]).reshape(1, 4 * hv)
    beta, gc = _dn_gates(ba, alog_row, dtb_row, hv)
    nc = t // DN_CHUNK
    beta_g = beta.reshape(b, 2, t, hv // g, g).transpose(0, 1, 3, 2, 4)
    gcc_g = gc.reshape(b, 2, t, hv // g, g).transpose(0, 1, 3, 2, 4)
    gcr_g = (gc.reshape(b, 2, nc, DN_CHUNK, hv // g, g).transpose(0, 1, 4, 2, 5, 3)
             .reshape(b, 2, hv // g, nc, g // 2, 2 * DN_CHUNK))
    o = _dn_chunks(qkv, beta_g, gcc_g, gcr_g, ctx_len, qk_w, hv)
    return _dn_finish(o, proj, onorm_w, conv_ch)


def _rope(x, cos, sin):
    quarter = HEAD_DIM // 4
    lane = lax.broadcasted_iota(jnp.int32, x.shape, 1)
    up = pltpu.roll(x, HEAD_DIM - quarter, 1)
    down = pltpu.roll(x, quarter, 1)
    rot = jnp.where((lane % (2 * quarter)) < quarter, -up, down)
    return x * cos + rot * sin


def _da_kernel(q_ref, k_ref, v_ref, g_ref, cos_ref, sin_ref, lam_ref, sw_ref, o_ref,
               krot_ref, q_scr, m_ref, l_ref, acc_ref, *, ctx_len, lambda_init, scale, tq, ck):
    qi = pl.program_id(2)
    t = k_ref.shape[1]
    maps = range(2)
    lanes = lambda j: slice(j * HEAD_DIM, (j + 1) * HEAD_DIM)
    qscale = scale * math.log2(math.e)
    lam = lam_ref[...]
    lam_full = (jnp.exp(jnp.sum(lam[0:1] * lam[1:2], axis=-1, keepdims=True))
                - jnp.exp(jnp.sum(lam[2:3] * lam[3:4], axis=-1, keepdims=True)) + lambda_init)

    def finish(o, gate):
        ms = jnp.mean(o * o, axis=-1, keepdims=True)
        y = o * lax.rsqrt(ms + DA_SUBLN_EPS) * sw_ref[...] * (1.0 - lambda_init)
        return (y * _silu(gate.astype(F32))).astype(BF16)

    @pl.when(qi == 0)
    def _():
        for j in maps:
            kk = k_ref[0, :, lanes(j)].astype(F32)
            krot_ref[:, lanes(j)] = _rope(kk, cos_ref[...], sin_ref[...]).astype(BF16)
        outs = []
        for j in maps:
            q = _rope(q_ref[0, 0:ctx_len, lanes(j)].astype(F32), cos_ref[0:ctx_len, :], sin_ref[0:ctx_len, :])
            s = _dot_nt((q * qscale).astype(BF16), krot_ref[0:ctx_len, lanes(j)])
            pr = jnp.exp2(s - jnp.max(s, axis=-1, keepdims=True))
            outs.append(_dot(pr.astype(BF16), v_ref[0, 0:ctx_len, :]) / jnp.sum(pr, axis=-1, keepdims=True))
        o_ref[0, 0:ctx_len, :] = finish(outs[0] - lam_full * outs[1], g_ref[0, 0:ctx_len, :])

    @pl.when(qi > 0)
    def _():
        r0 = pl.multiple_of(ctx_len + (qi - 1) * tq, math.gcd(ctx_len, tq))
        rows = pl.ds(r0, tq)
        for j in maps:
            q = _rope(q_ref[0, rows, lanes(j)].astype(F32), cos_ref[rows, :], sin_ref[rows, :])
            q_scr[j] = (q * qscale).astype(BF16)

        stat = (tq, HEAD_DIM)

        def widen(x, width):
            if width % HEAD_DIM == 0:
                return jnp.tile(x, (1, width // HEAD_DIM))
            return jnp.broadcast_to(x[:, 0:1], (tq, width))

        for j in maps:
            s = _dot_nt(q_scr[j], krot_ref[0:ctx_len, lanes(j)])
            m = jnp.broadcast_to(jnp.max(s, axis=-1, keepdims=True), stat)
            pr = jnp.exp2(s - widen(m, ctx_len))
            m_ref[j] = m
            l_ref[j] = jnp.broadcast_to(jnp.sum(pr, axis=-1, keepdims=True), stat)
            acc_ref[j] = _dot(pr.astype(BF16), v_ref[0, 0:ctx_len, :])

        def body(c, carry):
            k0 = pl.multiple_of(ctx_len + c * ck, math.gcd(ctx_len, ck))
            s = [_dot_nt(q_scr[j], krot_ref[pl.ds(k0, ck), lanes(j)]) for j in maps]
            m_old = [m_ref[j] for j in maps]
            m_new = [jnp.maximum(m_old[j], jnp.broadcast_to(jnp.max(s[j], axis=-1, keepdims=True), stat))
                     for j in maps]
            alpha = [jnp.exp2(m_old[j] - m_new[j]) for j in maps]
            pr = [jnp.exp2(s[j] - widen(m_new[j], ck)) for j in maps]
            v = v_ref[0, pl.ds(k0, ck), :]
            for j in maps:
                m_ref[j] = m_new[j]
                l_ref[j] = alpha[j] * l_ref[j] + jnp.broadcast_to(jnp.sum(pr[j], axis=-1, keepdims=True), stat)
                acc_ref[j] = widen(alpha[j], 2 * HEAD_DIM) * acc_ref[j] + _dot(pr[j].astype(BF16), v)
            return carry

        lax.fori_loop(0, (t - ctx_len) // ck, body, 0)
        o = (acc_ref[0] / widen(l_ref[0], 2 * HEAD_DIM)
             - lam_full * (acc_ref[1] / widen(l_ref[1], 2 * HEAD_DIM)))
        o_ref[0, rows, :] = finish(o, g_ref[0, rows, :])


def _rope_tables(ctx_len, n):
    rows = n // GRID_W
    half = HEAD_DIM // 2
    inv = ROPE_THETA ** (-jnp.arange(0, half, 2, dtype=F32) / half)
    ang_r = jnp.arange(rows, dtype=F32)[:, None] * inv
    ang_c = jnp.arange(GRID_W, dtype=F32)[:, None] * inv

    def table(fn, ctx_value):
        fr = jnp.broadcast_to(fn(ang_r)[:, None, :], (rows, GRID_W, half // 2))
        fc = jnp.broadcast_to(fn(ang_c)[None, :, :], (rows, GRID_W, half // 2))
        lat = jnp.concatenate([fr, fr, fc, fc], axis=-1).reshape(n, HEAD_DIM)
        return jnp.concatenate([jnp.full((ctx_len, HEAD_DIM), ctx_value, F32), lat], axis=0)

    return table(jnp.cos, 1.0), table(jnp.sin, 0.0)


def _diff_attn(proj, lam, subln_w, ctx_len, d_model, layer_idx):
    b, t, _ = proj.shape
    hw = 2 * HEAD_DIM
    heads = d_model // hw
    n = t - ctx_len
    tq = min(DA_QUERY_TILE, n)
    ck = min(DA_KEY_CHUNK, n)
    assert n % tq == 0 and n % ck == 0
    lambda_init = 0.8 - 0.6 * math.exp(-0.3 * layer_idx)
    cos, sin = _rope_tables(ctx_len, n)
    full = lambda bi, hi, qi: (0, 0)
    once = pl.Buffered(1)
    col = lambda off: (lambda bi, hi, qi: (bi, 0, off * heads + hi))
    return pl.pallas_call(
        functools.partial(_da_kernel, ctx_len=ctx_len, lambda_init=lambda_init, scale=HEAD_DIM ** -0.5,
                          tq=tq, ck=ck),
        grid=(b, heads, 1 + n // tq),
        in_specs=[pl.BlockSpec((1, t, hw), col(0)),
                  pl.BlockSpec((1, t, hw), col(1)),
                  pl.BlockSpec((1, t, hw), col(2)),
                  pl.BlockSpec((1, t, hw), col(3)),
                  pl.BlockSpec((t, HEAD_DIM), full, pipeline_mode=once),
                  pl.BlockSpec((t, HEAD_DIM), full, pipeline_mode=once),
                  pl.BlockSpec((4, HEAD_DIM), full),
                  pl.BlockSpec((1, hw), full)],
        out_specs=pl.BlockSpec((1, t, hw), col(0)),
        out_shape=jax.ShapeDtypeStruct((b, t, heads * hw), BF16),
        scratch_shapes=[pltpu.VMEM((t, hw), BF16),
                        pltpu.VMEM((2, tq, HEAD_DIM), BF16),
                        pltpu.VMEM((2, tq, HEAD_DIM), F32),
                        pltpu.VMEM((2, tq, HEAD_DIM), F32),
                        pltpu.VMEM((2, tq, hw), F32)],
        compiler_params=_params(("parallel", "parallel", "arbitrary")),
        name="diff_attn",
    )(proj, proj, proj, proj, cos, sin, lam, subln_w.reshape(1, hw))


def _na_kernel(q_ref, k_ref, v_ref, g_ref, bias_ref, o_ref, *, ctx_len, rows, scale):
    wk = NA_WR * GRID_W
    kc = k_ref[0, 0:ctx_len, :]
    vc = v_ref[0, 0:ctx_len, :]

    s = _dot_nt(q_ref[0, 0:ctx_len, :], kc) * scale
    m = jnp.max(s, axis=-1, keepdims=True)
    pr = jnp.exp(s - m)
    o = _dot(pr.astype(BF16), vc) / jnp.sum(pr, axis=-1, keepdims=True)
    o_ref[0, 0:ctx_len, :] = (o * _silu(g_ref[0, 0:ctx_len, :].astype(F32))).astype(BF16)

    def body(it, carry):
        nr = range(NA_ROWS_PER_ITER)
        r = [it * NA_ROWS_PER_ITER + i for i in nr]
        rs = [jnp.clip(r[i] - NA_WR // 2, 0, rows - NA_WR) for i in nr]
        qs = [pl.multiple_of(ctx_len + r[i] * GRID_W, GRID_W) for i in nr]
        ks = [pl.multiple_of(ctx_len + rs[i] * GRID_W, GRID_W) for i in nr]
        q = [q_ref[0, pl.ds(qs[i], GRID_W), :] for i in nr]
        s_win = [_dot_nt(q[i], k_ref[0, pl.ds(ks[i], wk), :]) * scale + bias_ref[0, rs[i] - r[i] + NA_WR - 1]
                 for i in nr]
        s_ctx = [_dot_nt(q[i], kc) * scale for i in nr]
        mx = [jnp.maximum(jnp.max(s_win[i], axis=-1, keepdims=True), jnp.max(s_ctx[i], axis=-1, keepdims=True))
              for i in nr]
        p_win = [jnp.exp(s_win[i] - mx[i]) for i in nr]
        p_ctx = [jnp.exp(s_ctx[i] - mx[i]) for i in nr]
        den = [jnp.sum(p_win[i], axis=-1, keepdims=True) + jnp.sum(p_ctx[i], axis=-1, keepdims=True) for i in nr]
        out = [(_dot(p_win[i].astype(BF16), v_ref[0, pl.ds(ks[i], wk), :]) + _dot(p_ctx[i].astype(BF16), vc)) / den[i]
               for i in nr]
        for i in nr:
            gate = g_ref[0, pl.ds(qs[i], GRID_W), :].astype(F32)
            o_ref[0, pl.ds(qs[i], GRID_W), :] = (out[i] * _silu(gate)).astype(BF16)
        return carry

    lax.fori_loop(0, rows // NA_ROWS_PER_ITER, body, 0)


def _na_bias_table(rpb):
    heads = rpb.shape[0]
    cols = jnp.arange(GRID_W)
    start = jnp.clip(cols - NA_WC // 2, 0, GRID_W - NA_WC)
    inside = (cols[None, :] >= start[:, None]) & (cols[None, :] < start[:, None] + NA_WC)
    dc = cols[None, :] - cols[:, None] + NA_WC - 1
    rpb = rpb.astype(F32)
    wide = jnp.full((heads, 2 * NA_WR - 1, GRID_W, GRID_W), NEG_BIG, F32)
    for off in range(2 * NA_WC - 1):
        wide = jnp.where((inside & (dc == off))[None, None], rpb[:, :, off, None, None], wide)
    tab = jnp.stack([wide[:, o:o + NA_WR] for o in range(NA_WR)], axis=1)
    return tab.transpose(0, 1, 3, 2, 4).reshape(heads, NA_WR, GRID_W, NA_WR * GRID_W)


def _neighbourhood(proj, rpb, ctx_len, d_model):
    b, t, _ = proj.shape
    heads = d_model // HEAD_DIM
    rows = (t - ctx_len) // GRID_W
    assert rows >= NA_WR
    wk = NA_WR * GRID_W
    bias = _na_bias_table(rpb)
    col = lambda off: (lambda bi, hi: (bi, 0, off * heads + hi))
    return pl.pallas_call(
        functools.partial(_na_kernel, ctx_len=ctx_len, rows=rows, scale=HEAD_DIM ** -0.5),
        grid=(b, heads),
        in_specs=[pl.BlockSpec((1, t, HEAD_DIM), col(0)),
                  pl.BlockSpec((1, t, HEAD_DIM), col(1)),
                  pl.BlockSpec((1, t, HEAD_DIM), col(2)),
                  pl.BlockSpec((1, t, HEAD_DIM), col(3)),
                  pl.BlockSpec((1, NA_WR, GRID_W, wk), lambda bi, hi: (hi, 0, 0, 0))],
        out_specs=pl.BlockSpec((1, t, HEAD_DIM), col(0)),
        out_shape=jax.ShapeDtypeStruct((b, t, heads * HEAD_DIM), BF16),
        compiler_params=_params(("parallel", "parallel")),
        name="neighbourhood",
    )(proj, proj, proj, proj, bias)


def kernel(x, c, ctx, c_ctx, norm_w, ada_w, ada_b, dn_w_in, dn_conv_w, dn_a_log, dn_dt_bias, dn_onorm_w,
           dn_w_out, da_w_in, da_lambda, da_subln_w, da_w_out, na_w_in, na_rpb, na_w_out, final_norm_w):
    b, n, d = x.shape
    ctx_len = ctx.shape[1]
    t = ctx_len + n
    depth = norm_w.shape[0]
    assert b + 1 <= MOD_ROWS and n % GRID_W == 0 and ctx_len % DN_CHUNK == 0
    tm = t // 4
    assert t % 4 == 0 and tm % 16 == 0

    xa = jnp.concatenate([ctx, x], axis=1)
    cvec = jnp.concatenate([c, c_ctx[None, :], jnp.zeros((MOD_ROWS - b - 1, d), F32)], axis=0)
    mods = _modulation(cvec, ada_w, ada_b)

    conv_ch = 4 * d
    for i in range(depth):
        kind, j = i % N_MIXERS, i // N_MIXERS
        sh, sc, gt = mods[i, :, 0:d], mods[i, :, d:2 * d], mods[i, :, 2 * d:3 * d]
        lat = lambda m: m[0:b, None, :]
        cx = lambda m: m[b:b + 1, :]
        if kind == 0:
            w_in = dn_w_in[j]
            proj, ba = _norm_proj(xa, norm_w[i], lat(sc), lat(sh), cx(sc), cx(sh),
                                  w_in[:, 0:conv_ch + 2 * d].astype(BF16), w_in[:, conv_ch + 2 * d:].astype(BF16),
                                  ctx_len, tm)
            y = _deltanet(proj, ba, dn_conv_w[j], dn_a_log[j], dn_dt_bias[j], dn_onorm_w[j], ctx_len, d)
            w_out = dn_w_out[j]
        elif kind == 1:
            proj, _ = _norm_proj(xa, norm_w[i], lat(sc), lat(sh), cx(sc), cx(sh),
                                 da_w_in[j].astype(BF16), None, ctx_len, tm)
            y = _diff_attn(proj, da_lambda[j], da_subln_w[j], ctx_len, d, i)
            w_out = da_w_out[j]
        else:
            proj, _ = _norm_proj(xa, norm_w[i], lat(sc), lat(sh), cx(sc), cx(sh),
                                 na_w_in[j].astype(BF16), None, ctx_len, tm)
            y = _neighbourhood(proj, na_rpb[j], ctx_len, d)
            w_out = na_w_out[j]
        xa = _out_proj(y, w_out.astype(BF16), xa, lat(gt), cx(gt), ctx_len, tm)
    return _final_norm(xa, final_norm_w, ctx_len)
```

```python
import functools
import math

import jax
import jax.numpy as jnp
from jax import lax
from jax.experimental import pallas as pl
from jax.experimental.pallas import tpu as pltpu

F32 = jnp.float32
BF16 = jnp.bfloat16

GRID_W = 64
N_MIXERS = 3
RMS_EPS = 1e-6
HEAD_DIM = 128
DN_CONV_K = 5
DN_CHUNK = 64
DN_HEADS_PER_STEP = 32
DA_SUBLN_EPS = 1e-5
DA_QUERY_TILE = 1024
DA_KEY_CHUNK = 1024
ROPE_THETA = 10000.0
NA_WR = 8
NA_WC = 16
NA_ROWS_PER_ITER = 16
NEG_BIG = -1e30
PROJ_TN = 512
MOD_ROWS = 8
V7X_VMEM_LIMIT = 56 * 1024 * 1024


def _sigmoid(x):
    return 1.0 / (1.0 + jnp.exp(-x))


def _silu(x):
    return x * _sigmoid(x)


def _dot(a, b):
    return jnp.dot(a, b, preferred_element_type=F32)


def _dot_nt(a, b):
    return lax.dot_general(a, b, (((1,), (1,)), ((), ())), preferred_element_type=F32)


def _dot_tn(a, b):
    return lax.dot_general(a, b, (((0,), (0,)), ((), ())), preferred_element_type=F32)


def _params(semantics, vmem=V7X_VMEM_LIMIT):
    return pltpu.CompilerParams(dimension_semantics=semantics, vmem_limit_bytes=vmem)


def _mod_kernel(c_ref, w_ref, b_ref, o_ref):
    s = _silu(c_ref[...])
    o_ref[0] = jnp.dot(s, w_ref[0], preferred_element_type=F32,
                       precision=lax.Precision.HIGHEST) + b_ref[0]


def _modulation(cvec, ada_w, ada_b):
    depth, d, p = ada_w.shape
    tn = 1024 if p % 1024 == 0 else p
    return pl.pallas_call(
        _mod_kernel,
        grid=(depth, p // tn),
        in_specs=[pl.BlockSpec((MOD_ROWS, d), lambda i, n: (0, 0)),
                  pl.BlockSpec((1, d, tn), lambda i, n: (i, 0, n)),
                  pl.BlockSpec((1, 1, tn), lambda i, n: (i, 0, n))],
        out_specs=pl.BlockSpec((1, MOD_ROWS, tn), lambda i, n: (i, 0, n)),
        out_shape=jax.ShapeDtypeStruct((depth, MOD_ROWS, p), F32),
        compiler_params=_params(("parallel", "parallel")),
        name="adaln_mod",
    )(cvec, ada_w, ada_b.reshape(depth, 1, p))


def _norm_proj_kernel(*refs, ctx_len, tm, has_extra):
    if has_extra:
        x_ref, nw_ref, scl_ref, shl_ref, scc_ref, shc_ref, w_ref, wx_ref, o_ref, ox_ref, h_ref = refs
    else:
        x_ref, nw_ref, scl_ref, shl_ref, scc_ref, shc_ref, w_ref, o_ref, h_ref = refs
    mi = pl.program_id(1)
    ni = pl.program_id(2)

    @pl.when(ni == 0)
    def _():
        x = x_ref[0]
        ms = jnp.mean(x * x, axis=-1, keepdims=True)
        y = x * lax.rsqrt(ms + RMS_EPS) * nw_ref[...]
        row = lax.broadcasted_iota(jnp.int32, (tm, 1), 0) + mi * tm
        is_ctx = row < ctx_len
        sc = jnp.where(is_ctx, scc_ref[...], scl_ref[0])
        sh = jnp.where(is_ctx, shc_ref[...], shl_ref[0])
        h = (y * (1.0 + sc) + sh).astype(BF16)
        h_ref[...] = h
        if has_extra:
            ox_ref[0] = _dot(h, wx_ref[...])

    o_ref[0] = _dot(h_ref[...], w_ref[...]).astype(BF16)


def _norm_proj(xa, nw, scl, shl, scc, shc, w, wx, ctx_len, tm):
    b, t, d = xa.shape
    p = w.shape[1]
    tn = PROJ_TN if p % PROJ_TN == 0 else PROJ_TN // 2
    assert t % tm == 0 and p % tn == 0
    has_extra = wx is not None
    vec_l = pl.BlockSpec((1, 1, d), lambda bi, mi, ni: (bi, 0, 0))
    vec_c = pl.BlockSpec((1, d), lambda bi, mi, ni: (0, 0))
    in_specs = [pl.BlockSpec((1, tm, d), lambda bi, mi, ni: (bi, mi, 0)),
                vec_c, vec_l, vec_l, vec_c, vec_c,
                pl.BlockSpec((d, tn), lambda bi, mi, ni: (0, ni))]
    out_specs = [pl.BlockSpec((1, tm, tn), lambda bi, mi, ni: (bi, mi, ni))]
    out_shape = [jax.ShapeDtypeStruct((b, t, p), BF16)]
    args = [xa, nw.reshape(1, d), scl, shl, scc, shc, w]
    if has_extra:
        px = wx.shape[1]
        in_specs.append(pl.BlockSpec((d, px), lambda bi, mi, ni: (0, 0)))
        out_specs.append(pl.BlockSpec((1, tm, px), lambda bi, mi, ni: (bi, mi, 0)))
        out_shape.append(jax.ShapeDtypeStruct((b, t, px), F32))
        args.append(wx)
    outs = pl.pallas_call(
        functools.partial(_norm_proj_kernel, ctx_len=ctx_len, tm=tm, has_extra=has_extra),
        grid=(b, t // tm, p // tn),
        in_specs=in_specs,
        out_specs=out_specs,
        out_shape=out_shape,
        scratch_shapes=[pltpu.VMEM((tm, d), BF16)],
        compiler_params=_params(("parallel", "parallel", "arbitrary")),
        name="norm_proj",
    )(*args)
    return (outs[0], outs[1]) if has_extra else (outs[0], None)


def _out_proj_kernel(y_ref, w_ref, x_ref, gtl_ref, gtc_ref, o_ref, *, ctx_len, tm):
    mi = pl.program_id(1)
    acc = _dot(y_ref[0], w_ref[...])
    row = lax.broadcasted_iota(jnp.int32, (tm, 1), 0) + mi * tm
    gt = jnp.where(row < ctx_len, gtc_ref[...], gtl_ref[0])
    o_ref[0] = x_ref[0] + gt * acc


def _out_proj(y, w, xa, gtl, gtc, ctx_len, tm):
    b, t, d = xa.shape
    kd = y.shape[2]
    tn = PROJ_TN if d % PROJ_TN == 0 else PROJ_TN // 2
    while 2 * (tm * kd * 2 + kd * tn * 2 + 2 * tm * tn * 4) > V7X_VMEM_LIMIT * 3 // 4:
        tn //= 2
    assert d % tn == 0 and t % tm == 0
    return pl.pallas_call(
        functools.partial(_out_proj_kernel, ctx_len=ctx_len, tm=tm),
        grid=(b, t // tm, d // tn),
        in_specs=[pl.BlockSpec((1, tm, kd), lambda bi, mi, ni: (bi, mi, 0)),
                  pl.BlockSpec((kd, tn), lambda bi, mi, ni: (0, ni)),
                  pl.BlockSpec((1, tm, tn), lambda bi, mi, ni: (bi, mi, ni)),
                  pl.BlockSpec((1, 1, tn), lambda bi, mi, ni: (bi, 0, ni)),
                  pl.BlockSpec((1, tn), lambda bi, mi, ni: (0, ni))],
        out_specs=pl.BlockSpec((1, tm, tn), lambda bi, mi, ni: (bi, mi, ni)),
        out_shape=jax.ShapeDtypeStruct((b, t, d), F32),
        compiler_params=_params(("parallel", "parallel", "arbitrary")),
        name="out_proj",
    )(y, w, xa, gtl, gtc)


def _final_norm_kernel(x_ref, w_ref, o_ref):
    x = x_ref[0]
    ms = jnp.mean(x * x, axis=-1, keepdims=True)
    o_ref[0] = x * lax.rsqrt(ms + RMS_EPS) * w_ref[...]


def _final_norm(xa, w, ctx_len):
    b, t, d = xa.shape
    n = t - ctx_len
    tr = math.gcd(ctx_len, 256)
    assert n % tr == 0
    off = ctx_len // tr
    return pl.pallas_call(
        _final_norm_kernel,
        grid=(b, n // tr),
        in_specs=[pl.BlockSpec((1, tr, d), lambda bi, ri: (bi, ri + off, 0)),
                  pl.BlockSpec((1, d), lambda bi, ri: (0, 0))],
        out_specs=pl.BlockSpec((1, tr, d), lambda bi, ri: (bi, ri, 0)),
        out_shape=jax.ShapeDtypeStruct((b, n, d), F32),
        compiler_params=_params(("parallel", "parallel")),
        name="final_norm",
    )(xa, w.reshape(1, d))


def _dn_conv_kernel(x_ref, w_ref, o_ref, *, ctx_len, n_qk_blocks):
    blk = pl.program_id(1)
    x = x_ref[0].astype(F32)
    t = x.shape[0]
    w = w_ref[...]
    row = lax.broadcasted_iota(jnp.int32, x.shape, 0)
    seg = row >= ctx_len
    half = DN_CONV_K // 2
    acc = x * w[half:half + 1, :]
    for tap in range(DN_CONV_K):
        dlt = tap - half
        if dlt == 0:
            continue
        shifted = pltpu.roll(x, (-dlt) % t, 0)
        src = row + dlt
        ok = (src >= 0) & (src < t) & ((src >= ctx_len) == seg)
        acc = acc + jnp.where(ok, shifted, 0.0) * w[tap:tap + 1, :]
    y = _silu(acc)
    inv = lax.rsqrt(jnp.sum(y * y, axis=-1, keepdims=True) + 1e-6)
    y = y * jnp.where(blk < n_qk_blocks, inv, 1.0)
    o_ref[0] = y.astype(BF16)


def _dn_conv(proj, conv_w_t, ctx_len, conv_ch, n_qk_blocks):
    b, t, _ = proj.shape
    return pl.pallas_call(
        functools.partial(_dn_conv_kernel, ctx_len=ctx_len, n_qk_blocks=n_qk_blocks),
        grid=(b, conv_ch // HEAD_DIM),
        in_specs=[pl.BlockSpec((1, t, HEAD_DIM), lambda bi, ci: (bi, 0, ci)),
                  pl.BlockSpec((DN_CONV_K, HEAD_DIM), lambda bi, ci: (0, ci))],
        out_specs=pl.BlockSpec((1, t, HEAD_DIM), lambda bi, ci: (bi, 0, ci)),
        out_shape=jax.ShapeDtypeStruct((b, t, conv_ch), BF16),
        compiler_params=_params(("parallel", "parallel")),
        name="dn_conv",
    )(proj, conv_w_t)


def _exact_f32_dot(tri, g):
    hi = g.astype(BF16)
    r1 = g - hi.astype(F32)
    mid = r1.astype(BF16)
    lo = (r1 - mid.astype(F32)).astype(BF16)
    return _dot(tri, hi) + _dot(tri, mid) + _dot(tri, lo)


def _dn_gate_kernel(ba_ref, alog_ref, dtb_ref, beta_ref, gc_ref, *, hv):
    ii = lax.broadcasted_iota(jnp.int32, (DN_CHUNK, DN_CHUNK), 0)
    jj = lax.broadcasted_iota(jnp.int32, (DN_CHUNK, DN_CHUNK), 1)
    lower = jnp.where(ii >= jj, 1.0, 0.0).astype(BF16)
    upper = jnp.where(ii <= jj, 1.0, 0.0).astype(BF16)
    ba = ba_ref[0]
    z = ba + dtb_ref[...]
    softplus = jnp.maximum(z, 0.0) + jnp.log1p(jnp.exp(-jnp.abs(z)))
    g = -jnp.exp(alog_ref[...]) * softplus
    beta = _sigmoid(ba)
    beta_ref[0, 0] = beta[:, 0:hv]
    beta_ref[0, 1] = beta[:, 2 * hv:3 * hv]
    for ci in range(ba.shape[0] // DN_CHUNK):
        rows = slice(ci * DN_CHUNK, (ci + 1) * DN_CHUNK)
        gc_ref[0, 0, rows, :] = _exact_f32_dot(lower, g[rows])[:, hv:2 * hv]
        gc_ref[0, 1, rows, :] = _exact_f32_dot(upper, g[rows])[:, 3 * hv:4 * hv]


def _dn_gates(ba, alog_row, dtb_row, hv):
    b, t, pw = ba.shape
    out = jax.ShapeDtypeStruct((b, 2, t, hv), F32)
    rows = next(r * DN_CHUNK for r in (4, 2, 1) if t % (r * DN_CHUNK) == 0)
    ospec = pl.BlockSpec((1, 2, rows, hv), lambda bi, ci: (bi, 0, ci, 0))
    return pl.pallas_call(
        functools.partial(_dn_gate_kernel, hv=hv),
        grid=(b, t // rows),
        in_specs=[pl.BlockSpec((1, rows, pw), lambda bi, ci: (bi, ci, 0)),
                  pl.BlockSpec((1, pw), lambda bi, ci: (0, 0)),
                  pl.BlockSpec((1, pw), lambda bi, ci: (0, 0))],
        out_specs=[ospec, ospec],
        out_shape=[out, out],
        compiler_params=_params(("parallel", "parallel")),
        name="dn_gates",
    )(ba, alog_row, dtb_row)


def _dn_chunk_kernel(q_ref, k_ref, v_ref, beta_ref, gcc_ref, gcr_ref, o_ref, s_ref, *, heads, scale):
    d = pl.program_id(2)
    p = pl.program_id(3)

    @pl.when(p == 0)
    def _():
        s_ref[...] = jnp.zeros_like(s_ref)

    c = DN_CHUNK
    ii = lax.broadcasted_iota(jnp.int32, (c, 2 * c), 0)
    lane = lax.broadcasted_iota(jnp.int32, (c, 2 * c), 1)
    jj = lane % c
    left = lane < c
    order = (ii - jj) * (1 - 2 * d)
    incl = order >= 0
    strict = order > 0
    eye = jnp.where(ii == jj, 1.0, 0.0)
    pair_masks = []
    sz = 1
    while sz < c:
        pair_masks.append((ii // (2 * sz) == jj // (2 * sz)) & (ii // sz != jj // sz))
        sz *= 2
    fwd = d == 0
    hr = range(heads)
    pr = range(heads // 2)
    lanes = lambda i: slice(i * HEAD_DIM, (i + 1) * HEAD_DIM)
    zeros_bf = jnp.zeros((c, HEAD_DIM), BF16)

    def block_diag(x):
        return jnp.concatenate([jnp.where(left, x, 0.0), jnp.where(left, 0.0, x)], axis=0).astype(BF16)

    def block_diag_wide(a, b):
        return jnp.concatenate([jnp.concatenate([a, zeros_bf], axis=1),
                                jnp.concatenate([zeros_bf, b], axis=1)], axis=0)

    kb, kq, kk, qk = [], [], [], []
    for m in pr:
        q = q_ref[0, :, lanes(m)]
        k = k_ref[0, :, lanes(m)]
        kq.append(jnp.concatenate([k, q], axis=0))
        both = _dot_nt(kq[m], jnp.concatenate([k, k], axis=0))
        kk.append(both[0:c])
        qk.append(both[c:2 * c] * scale)
        kb.append(k)
    beta = [beta_ref[0, 0, 0, :, h:h + 1] for h in hr]
    gcol = [gcc_ref[0, 0, 0, :, h:h + 1] for h in hr]
    beta2 = [jnp.where(left, beta[2 * m], beta[2 * m + 1]) for m in pr]
    gcol2 = [jnp.where(left, gcol[2 * m], gcol[2 * m + 1]) for m in pr]
    grow2 = [gcr_ref[0, 0, 0, 0, m:m + 1, :] for m in pr]
    gam = [jnp.where(incl, jnp.exp(jnp.where(incl, gcol2[m] - grow2[m], 0.0)), 0.0) for m in pr]
    low = [jnp.where(strict, beta2[m] * kk[m] * gam[m], 0.0) for m in pr]
    tinv = [eye - jnp.where(pair_masks[0], low[m], 0.0) for m in pr]
    for pm in pair_masks[1:]:
        xc = [_dot(tinv[m].astype(BF16), block_diag(jnp.where(pm, low[m], 0.0))) for m in pr]
        tinv = [tinv[m] - _dot(xc[m].astype(BF16), block_diag(tinv[m])) for m in pr]
    tb = [tinv[m].astype(BF16) for m in pr]
    eg = [jnp.exp(gcol[h]) for h in hr]
    glast = [jnp.where(fwd, gcol[h][c - 1:c, :], gcol[h][0:1, :]) for h in hr]
    s_old = [s_ref[h] for h in hr]
    half = lambda h: lanes(h % 2)
    kqs = [_dot(kq[m], jnp.concatenate([s_old[2 * m].astype(BF16), s_old[2 * m + 1].astype(BF16)], axis=1))
           for m in pr]
    rhs = [(beta[h] * (v_ref[0, :, lanes(h)].astype(F32) - eg[h] * kqs[h // 2][0:c, half(h)])).astype(BF16)
           for h in hr]
    v_new = [_dot(tb[m], block_diag_wide(rhs[2 * m], rhs[2 * m + 1])) for m in pr]
    vnb = [v_new[m].astype(BF16) for m in pr]
    for m in pr:
        intra = _dot((qk[m] * gam[m]).astype(BF16),
                     block_diag_wide(vnb[m][:, 0:HEAD_DIM], vnb[m][:, HEAD_DIM:2 * HEAD_DIM]))
        inter = jnp.concatenate([(scale * eg[h]) * kqs[m][c:2 * c, half(h)] for h in (2 * m, 2 * m + 1)], axis=1)
        o_ref[0, 0, :, 2 * m * HEAD_DIM:(2 * m + 2) * HEAD_DIM] = (inter + intra).astype(BF16)
    wv = [jnp.concatenate([jnp.exp(glast[h] - gcol[h]) * v_new[m][:, half(h)] for h in (2 * m, 2 * m + 1)],
                          axis=1).astype(BF16) for m in pr]
    upd = [_dot_tn(kb[m], wv[m]) for m in pr]
    for h in hr:
        s_ref[h] = s_old[h] * jnp.exp(glast[h]) + upd[h // 2][:, half(h)]


def _dn_chunks(qkv, beta_g, gcc_g, gcr_g, ctx_len, qk_w, hv):
    b, t, _ = qkv.shape
    g = beta_g.shape[-1]
    assert hv % g == 0 and qk_w % (g // 2 * HEAD_DIM) == 0
    nc = t // DN_CHUNK
    nctx = ctx_len // DN_CHUNK
    qw = g // 2 * HEAD_DIM
    vw = g * HEAD_DIM
    k_off = qk_w // qw
    v_off = 2 * qk_w // vw

    def chunk(d, p):
        back = jnp.where(p < nctx, nctx - 1 - p, nc + nctx - 1 - p)
        return jnp.where(d == 0, p, back)

    return pl.pallas_call(
        functools.partial(_dn_chunk_kernel, heads=g, scale=HEAD_DIM ** -0.5),
        grid=(b, hv // g, 2, nc),
        in_specs=[pl.BlockSpec((1, DN_CHUNK, qw), lambda bi, gi, d, p: (bi, chunk(d, p), gi)),
                  pl.BlockSpec((1, DN_CHUNK, qw), lambda bi, gi, d, p: (bi, chunk(d, p), k_off + gi)),
                  pl.BlockSpec((1, DN_CHUNK, vw), lambda bi, gi, d, p: (bi, chunk(d, p), v_off + gi)),
                  pl.BlockSpec((1, 1, 1, DN_CHUNK, g), lambda bi, gi, d, p: (bi, d, gi, chunk(d, p), 0)),
                  pl.BlockSpec((1, 1, 1, DN_CHUNK, g), lambda bi, gi, d, p: (bi, d, gi, chunk(d, p), 0)),
                  pl.BlockSpec((1, 1, 1, 1, g // 2, 2 * DN_CHUNK),
                               lambda bi, gi, d, p: (bi, d, gi, chunk(d, p), 0, 0))],
        out_specs=pl.BlockSpec((1, 1, DN_CHUNK, vw), lambda bi, gi, d, p: (bi, d, chunk(d, p), gi)),
        out_shape=jax.ShapeDtypeStruct((b, 2, t, hv * HEAD_DIM), BF16),
        scratch_shapes=[pltpu.VMEM((g, HEAD_DIM, HEAD_DIM), F32)],
        compiler_params=_params(("parallel", "parallel", "arbitrary", "arbitrary")),
        name="dn_chunks",
    )(qkv, qkv, qkv, beta_g, gcc_g, gcr_g)


def _dn_finish_kernel(o_ref, z_ref, w_ref, y_ref):
    cw = y_ref.shape[2]
    for h in range(cw // HEAD_DIM):
        hl = slice(h * HEAD_DIM, (h + 1) * HEAD_DIM)
        o = o_ref[0, 0, :, hl].astype(F32) + o_ref[0, 1, :, hl].astype(F32)
        ms = jnp.mean(o * o, axis=-1, keepdims=True)
        y = o * lax.rsqrt(ms + RMS_EPS) * w_ref[...]
        y_ref[0, :, hl] = (y * _silu(z_ref[0, :, hl].astype(F32))).astype(BF16)


def _dn_finish(o, proj, onorm_w, z_off):
    b, _, t, vw = o.shape
    cw = 1024 if vw % 1024 == 0 else vw
    assert z_off % cw == 0
    tr = math.gcd(t, 256)
    return pl.pallas_call(
        _dn_finish_kernel,
        grid=(b, t // tr, vw // cw),
        in_specs=[pl.BlockSpec((1, 2, tr, cw), lambda bi, ri, ci: (bi, 0, ri, ci)),
                  pl.BlockSpec((1, tr, cw), lambda bi, ri, ci: (bi, ri, z_off // cw + ci)),
                  pl.BlockSpec((1, HEAD_DIM), lambda bi, ri, ci: (0, 0))],
        out_specs=pl.BlockSpec((1, tr, cw), lambda bi, ri, ci: (bi, ri, ci)),
        out_shape=jax.ShapeDtypeStruct((b, t, vw), BF16),
        compiler_params=_params(("parallel", "parallel", "parallel")),
        name="dn_finish",
    )(o, proj, onorm_w.reshape(1, HEAD_DIM))


def _deltanet(proj, ba, conv_w, a_log, dt_bias, onorm_w, ctx_len, d_model):
    hk = d_model // HEAD_DIM
    hv = 2 * hk
    qk_w = hk * HEAD_DIM
    conv_ch = 2 * qk_w + hv * HEAD_DIM
    b, t, _ = proj.shape
    g = min(DN_HEADS_PER_STEP, hv)
    qkv = _dn_conv(proj, jnp.transpose(conv_w), ctx_len, conv_ch, 2 * hk)
    zeros = jnp.zeros((hv,), F32)
    alog_row = jnp.concatenate([zeros, a_log[0], zeros, a_log[1]]).reshape(1, 4 * hv)
    dtb_row = jnp.concatenate([zeros, dt_bias[0], zeros, dt_bias[1]]).reshape(1, 4 * hv)
    beta, gc = _dn_gates(ba, alog_row, dtb_row, hv)
    nc = t // DN_CHUNK
    beta_g = beta.reshape(b, 2, t, hv // g, g).transpose(0, 1, 3, 2, 4)
    gcc_g = gc.reshape(b, 2, t, hv // g, g).transpose(0, 1, 3, 2, 4)
    gcr_g = (gc.reshape(b, 2, nc, DN_CHUNK, hv // g, g).transpose(0, 1, 4, 2, 5, 3)
             .reshape(b, 2, hv // g, nc, g // 2, 2 * DN_CHUNK))
    o = _dn_chunks(qkv, beta_g, gcc_g, gcr_g, ctx_len, qk_w, hv)
    return _dn_finish(o, proj, onorm_w, conv_ch)


def _rope(x, cos, sin):
    quarter = HEAD_DIM // 4
    lane = lax.broadcasted_iota(jnp.int32, x.shape, 1)
    up = pltpu.roll(x, HEAD_DIM - quarter, 1)
    down = pltpu.roll(x, quarter, 1)
    rot = jnp.where((lane % (2 * quarter)) < quarter, -up, down)
    return x * cos + rot * sin


def _da_kernel(q_ref, k_ref, v_ref, g_ref, cos_ref, sin_ref, lam_ref, sw_ref, o_ref,
               krot_ref, q_scr, m_ref, l_ref, acc_ref, *, ctx_len, lambda_init, scale, tq, ck):
    qi = pl.program_id(2)
    t = k_ref.shape[1]
    maps = range(2)
    lanes = lambda j: slice(j * HEAD_DIM, (j + 1) * HEAD_DIM)
    qscale = scale * math.log2(math.e)
    lam = lam_ref[...]
    lam_full = (jnp.exp(jnp.sum(lam[0:1] * lam[1:2], axis=-1, keepdims=True))
                - jnp.exp(jnp.sum(lam[2:3] * lam[3:4], axis=-1, keepdims=True)) + lambda_init)

    def finish(o, gate):
        ms = jnp.mean(o * o, axis=-1, keepdims=True)
        y = o * lax.rsqrt(ms + DA_SUBLN_EPS) * sw_ref[...] * (1.0 - lambda_init)
        return (y * _silu(gate.astype(F32))).astype(BF16)

    @pl.when(qi == 0)
    def _():
        for j in maps:
            kk = k_ref[0, :, lanes(j)].astype(F32)
            krot_ref[:, lanes(j)] = _rope(kk, cos_ref[...], sin_ref[...]).astype(BF16)
        outs = []
        for j in maps:
            q = _rope(q_ref[0, 0:ctx_len, lanes(j)].astype(F32), cos_ref[0:ctx_len, :], sin_ref[0:ctx_len, :])
            s = _dot_nt((q * qscale).astype(BF16), krot_ref[0:ctx_len, lanes(j)])
            pr = jnp.exp2(s - jnp.max(s, axis=-1, keepdims=True))
            outs.append(_dot(pr.astype(BF16), v_ref[0, 0:ctx_len, :]) / jnp.sum(pr, axis=-1, keepdims=True))
        o_ref[0, 0:ctx_len, :] = finish(outs[0] - lam_full * outs[1], g_ref[0, 0:ctx_len, :])

    @pl.when(qi > 0)
    def _():
        r0 = pl.multiple_of(ctx_len + (qi - 1) * tq, math.gcd(ctx_len, tq))
        rows = pl.ds(r0, tq)
        for j in maps:
            q = _rope(q_ref[0, rows, lanes(j)].astype(F32), cos_ref[rows, :], sin_ref[rows, :])
            q_scr[j] = (q * qscale).astype(BF16)

        stat = (tq, HEAD_DIM)

        def widen(x, width):
            if width % HEAD_DIM == 0:
                return jnp.tile(x, (1, width // HEAD_DIM))
            return jnp.broadcast_to(x[:, 0:1], (tq, width))

        for j in maps:
            s = _dot_nt(q_scr[j], krot_ref[0:ctx_len, lanes(j)])
            m = jnp.broadcast_to(jnp.max(s, axis=-1, keepdims=True), stat)
            pr = jnp.exp2(s - widen(m, ctx_len))
            m_ref[j] = m
            l_ref[j] = jnp.broadcast_to(jnp.sum(pr, axis=-1, keepdims=True), stat)
            acc_ref[j] = _dot(pr.astype(BF16), v_ref[0, 0:ctx_len, :])

        def body(c, carry):
            k0 = pl.multiple_of(ctx_len + c * ck, math.gcd(ctx_len, ck))
            s = [_dot_nt(q_scr[j], krot_ref[pl.ds(k0, ck), lanes(j)]) for j in maps]
            m_old = [m_ref[j] for j in maps]
            m_new = [jnp.maximum(m_old[j], jnp.broadcast_to(jnp.max(s[j], axis=-1, keepdims=True), stat))
                     for j in maps]
            alpha = [jnp.exp2(m_old[j] - m_new[j]) for j in maps]
            pr = [jnp.exp2(s[j] - widen(m_new[j], ck)) for j in maps]
            v = v_ref[0, pl.ds(k0, ck), :]
            for j in maps:
                m_ref[j] = m_new[j]
                l_ref[j] = alpha[j] * l_ref[j] + jnp.broadcast_to(jnp.sum(pr[j], axis=-1, keepdims=True), stat)
                acc_ref[j] = widen(alpha[j], 2 * HEAD_DIM) * acc_ref[j] + _dot(pr[j].astype(BF16), v)
            return carry

        lax.fori_loop(0, (t - ctx_len) // ck, body, 0)
        o = (acc_ref[0] / widen(l_ref[0], 2 * HEAD_DIM)
             - lam_full * (acc_ref[1] / widen(l_ref[1], 2 * HEAD_DIM)))
        o_ref[0, rows, :] = finish(o, g_ref[0, rows, :])


def _rope_tables(ctx_len, n):
    rows = n // GRID_W
    half = HEAD_DIM // 2
    inv = ROPE_THETA ** (-jnp.arange(0, half, 2, dtype=F32) / half)
    ang_r = jnp.arange(rows, dtype=F32)[:, None] * inv
    ang_c = jnp.arange(GRID_W, dtype=F32)[:, None] * inv

    def table(fn, ctx_value):
        fr = jnp.broadcast_to(fn(ang_r)[:, None, :], (rows, GRID_W, half // 2))
        fc = jnp.broadcast_to(fn(ang_c)[None, :, :], (rows, GRID_W, half // 2))
        lat = jnp.concatenate([fr, fr, fc, fc], axis=-1).reshape(n, HEAD_DIM)
        return jnp.concatenate([jnp.full((ctx_len, HEAD_DIM), ctx_value, F32), lat], axis=0)

    return table(jnp.cos, 1.0), table(jnp.sin, 0.0)


def _diff_attn(proj, lam, subln_w, ctx_len, d_model, layer_idx):
    b, t, _ = proj.shape
    hw = 2 * HEAD_DIM
    heads = d_model // hw
    n = t - ctx_len
    tq = min(DA_QUERY_TILE, n)
    ck = min(DA_KEY_CHUNK, n)
    assert n % tq == 0 and n % ck == 0
    lambda_init = 0.8 - 0.6 * math.exp(-0.3 * layer_idx)
    cos, sin = _rope_tables(ctx_len, n)
    full = lambda bi, hi, qi: (0, 0)
    once = pl.Buffered(1)
    col = lambda off: (lambda bi, hi, qi: (bi, 0, off * heads + hi))
    return pl.pallas_call(
        functools.partial(_da_kernel, ctx_len=ctx_len, lambda_init=lambda_init, scale=HEAD_DIM ** -0.5,
                          tq=tq, ck=ck),
        grid=(b, heads, 1 + n // tq),
        in_specs=[pl.BlockSpec((1, t, hw), col(0)),
                  pl.BlockSpec((1, t, hw), col(1)),
                  pl.BlockSpec((1, t, hw), col(2)),
                  pl.BlockSpec((1, t, hw), col(3)),
                  pl.BlockSpec((t, HEAD_DIM), full, pipeline_mode=once),
                  pl.BlockSpec((t, HEAD_DIM), full, pipeline_mode=once),
                  pl.BlockSpec((4, HEAD_DIM), full),
                  pl.BlockSpec((1, hw), full)],
        out_specs=pl.BlockSpec((1, t, hw), col(0)),
        out_shape=jax.ShapeDtypeStruct((b, t, heads * hw), BF16),
        scratch_shapes=[pltpu.VMEM((t, hw), BF16),
                        pltpu.VMEM((2, tq, HEAD_DIM), BF16),
                        pltpu.VMEM((2, tq, HEAD_DIM), F32),
                        pltpu.VMEM((2, tq, HEAD_DIM), F32),
                        pltpu.VMEM((2, tq, hw), F32)],
        compiler_params=_params(("parallel", "parallel", "arbitrary")),
        name="diff_attn",
    )(proj, proj, proj, proj, cos, sin, lam, subln_w.reshape(1, hw))


def _na_kernel(q_ref, k_ref, v_ref, g_ref, bias_ref, o_ref, *, ctx_len, rows, scale):
    wk = NA_WR * GRID_W
    kc = k_ref[0, 0:ctx_len, :]
    vc = v_ref[0, 0:ctx_len, :]

    s = _dot_nt(q_ref[0, 0:ctx_len, :], kc) * scale
    m = jnp.max(s, axis=-1, keepdims=True)
    pr = jnp.exp(s - m)
    o = _dot(pr.astype(BF16), vc) / jnp.sum(pr, axis=-1, keepdims=True)
    o_ref[0, 0:ctx_len, :] = (o * _silu(g_ref[0, 0:ctx_len, :].astype(F32))).astype(BF16)

    per_iter = math.gcd(rows, NA_ROWS_PER_ITER)

    def body(it, carry):
        nr = range(per_iter)
        r = [it * per_iter + i for i in nr]
        rs = [jnp.clip(r[i] - NA_WR // 2, 0, rows - NA_WR) for i in nr]
        qs = [pl.multiple_of(ctx_len + r[i] * GRID_W, GRID_W) for i in nr]
        ks = [pl.multiple_of(ctx_len + rs[i] * GRID_W, GRID_W) for i in nr]
        q = [q_ref[0, pl.ds(qs[i], GRID_W), :] for i in nr]
        s_win = [_dot_nt(q[i], k_ref[0, pl.ds(ks[i], wk), :]) * scale + bias_ref[0, rs[i] - r[i] + NA_WR - 1]
                 for i in nr]
        s_ctx = [_dot_nt(q[i], kc) * scale for i in nr]
        mx = [jnp.maximum(jnp.max(s_win[i], axis=-1, keepdims=True), jnp.max(s_ctx[i], axis=-1, keepdims=True))
              for i in nr]
        p_win = [jnp.exp(s_win[i] - mx[i]) for i in nr]
        p_ctx = [jnp.exp(s_ctx[i] - mx[i]) for i in nr]
        den = [jnp.sum(p_win[i], axis=-1, keepdims=True) + jnp.sum(p_ctx[i], axis=-1, keepdims=True) for i in nr]
        out = [(_dot(p_win[i].astype(BF16), v_ref[0, pl.ds(ks[i], wk), :]) + _dot(p_ctx[i].astype(BF16), vc)) / den[i]
               for i in nr]
        for i in nr:
            gate = g_ref[0, pl.ds(qs[i], GRID_W), :].astype(F32)
            o_ref[0, pl.ds(qs[i], GRID_W), :] = (out[i] * _silu(gate)).astype(BF16)
        return carry

    lax.fori_loop(0, rows // per_iter, body, 0)


def _na_bias_table(rpb):
    heads = rpb.shape[0]
    cols = jnp.arange(GRID_W)
    start = jnp.clip(cols - NA_WC // 2, 0, GRID_W - NA_WC)
    inside = (cols[None, :] >= start[:, None]) & (cols[None, :] < start[:, None] + NA_WC)
    dc = cols[None, :] - cols[:, None] + NA_WC - 1
    rpb = rpb.astype(F32)
    wide = jnp.full((heads, 2 * NA_WR - 1, GRID_W, GRID_W), NEG_BIG, F32)
    for off in range(2 * NA_WC - 1):
        wide = jnp.where((inside & (dc == off))[None, None], rpb[:, :, off, None, None], wide)
    tab = jnp.stack([wide[:, o:o + NA_WR] for o in range(NA_WR)], axis=1)
    return tab.transpose(0, 1, 3, 2, 4).reshape(heads, NA_WR, GRID_W, NA_WR * GRID_W)


def _neighbourhood(proj, rpb, ctx_len, d_model):
    b, t, _ = proj.shape
    heads = d_model // HEAD_DIM
    rows = (t - ctx_len) // GRID_W
    assert rows >= NA_WR
    wk = NA_WR * GRID_W
    bias = _na_bias_table(rpb)
    col = lambda off: (lambda bi, hi: (bi, 0, off * heads + hi))
    return pl.pallas_call(
        functools.partial(_na_kernel, ctx_len=ctx_len, rows=rows, scale=HEAD_DIM ** -0.5),
        grid=(b, heads),
        in_specs=[pl.BlockSpec((1, t, HEAD_DIM), col(0)),
                  pl.BlockSpec((1, t, HEAD_DIM), col(1)),
                  pl.BlockSpec((1, t, HEAD_DIM), col(2)),
                  pl.BlockSpec((1, t, HEAD_DIM), col(3)),
                  pl.BlockSpec((1, NA_WR, GRID_W, wk), lambda bi, hi: (hi, 0, 0, 0))],
        out_specs=pl.BlockSpec((1, t, HEAD_DIM), col(0)),
        out_shape=jax.ShapeDtypeStruct((b, t, heads * HEAD_DIM), BF16),
        compiler_params=_params(("parallel", "parallel")),
        name="neighbourhood",
    )(proj, proj, proj, proj, bias)


def kernel(x, c, ctx, c_ctx, norm_w, ada_w, ada_b, dn_w_in, dn_conv_w, dn_a_log, dn_dt_bias, dn_onorm_w,
           dn_w_out, da_w_in, da_lambda, da_subln_w, da_w_out, na_w_in, na_rpb, na_w_out, final_norm_w):
    b, n, d = x.shape
    ctx_len = ctx.shape[1]
    t = ctx_len + n
    depth = norm_w.shape[0]
    assert b + 1 <= MOD_ROWS and n % GRID_W == 0 and ctx_len % DN_CHUNK == 0
    tm = t // 4
    assert t % 4 == 0 and tm % 16 == 0

    xa = jnp.concatenate([ctx, x], axis=1)
    cvec = jnp.concatenate([c, c_ctx[None, :], jnp.zeros((MOD_ROWS - b - 1, d), F32)], axis=0)
    mods = _modulation(cvec, ada_w, ada_b)

    conv_ch = 4 * d
    for i in range(depth):
        kind, j = i % N_MIXERS, i // N_MIXERS
        sh, sc, gt = mods[i, :, 0:d], mods[i, :, d:2 * d], mods[i, :, 2 * d:3 * d]
        lat = lambda m: m[0:b, None, :]
        cx = lambda m: m[b:b + 1, :]
        if kind == 0:
            w_in = dn_w_in[j]
            proj, ba = _norm_proj(xa, norm_w[i], lat(sc), lat(sh), cx(sc), cx(sh),
                                  w_in[:, 0:conv_ch + 2 * d].astype(BF16), w_in[:, conv_ch + 2 * d:].astype(BF16),
                                  ctx_len, tm)
            y = _deltanet(proj, ba, dn_conv_w[j], dn_a_log[j], dn_dt_bias[j], dn_onorm_w[j], ctx_len, d)
            w_out = dn_w_out[j]
        elif kind == 1:
            proj, _ = _norm_proj(xa, norm_w[i], lat(sc), lat(sh), cx(sc), cx(sh),
                                 da_w_in[j].astype(BF16), None, ctx_len, tm)
            y = _diff_attn(proj, da_lambda[j], da_subln_w[j], ctx_len, d, i)
            w_out = da_w_out[j]
        else:
            proj, _ = _norm_proj(xa, norm_w[i], lat(sc), lat(sh), cx(sc), cx(sh),
                                 na_w_in[j].astype(BF16), None, ctx_len, tm)
            y = _neighbourhood(proj, na_rpb[j], ctx_len, d)
            w_out = na_w_out[j]
        xa = _out_proj(y, w_out.astype(BF16), xa, lat(gt), cx(gt), ctx_len, tm)
    return _final_norm(xa, final_norm_w, ctx_len)
```

```python
import functools
import math

import jax
import jax.numpy as jnp
from jax import lax
from jax.experimental import pallas as pl
from jax.experimental.pallas import tpu as pltpu

F32 = jnp.float32
BF16 = jnp.bfloat16

GRID_W = 64
N_MIXERS = 3
RMS_EPS = 1e-6
HEAD_DIM = 128
DN_CONV_K = 5
DN_CHUNK = 64
DN_HEADS_PER_STEP = 32
DA_SUBLN_EPS = 1e-5
DA_QUERY_TILE = 1024
DA_KEY_CHUNK = 1024
ROPE_THETA = 10000.0
NA_WR = 8
NA_WC = 16
NA_ROWS_PER_ITER = 16
NEG_BIG = -1e30
PROJ_TN = 512
MOD_ROWS = 8
V7X_VMEM_LIMIT = 56 * 1024 * 1024


def _sigmoid(x):
    return 1.0 / (1.0 + jnp.exp(-x))


def _silu(x):
    return x * _sigmoid(x)


def _dot(a, b):
    return jnp.dot(a, b, preferred_element_type=F32)


def _dot_nt(a, b):
    return lax.dot_general(a, b, (((1,), (1,)), ((), ())), preferred_element_type=F32)


def _dot_tn(a, b):
    return lax.dot_general(a, b, (((0,), (0,)), ((), ())), preferred_element_type=F32)


def _params(semantics, vmem=V7X_VMEM_LIMIT):
    return pltpu.CompilerParams(dimension_semantics=semantics, vmem_limit_bytes=vmem)


def _mod_kernel(c_ref, w_ref, b_ref, o_ref):
    s = _silu(c_ref[...])
    o_ref[0] = jnp.dot(s, w_ref[0], preferred_element_type=F32,
                       precision=lax.Precision.HIGHEST) + b_ref[0]


def _modulation(cvec, ada_w, ada_b):
    depth, d, p = ada_w.shape
    tn = 1024 if p % 1024 == 0 else p
    return pl.pallas_call(
        _mod_kernel,
        grid=(depth, p // tn),
        in_specs=[pl.BlockSpec((MOD_ROWS, d), lambda i, n: (0, 0)),
                  pl.BlockSpec((1, d, tn), lambda i, n: (i, 0, n)),
                  pl.BlockSpec((1, 1, tn), lambda i, n: (i, 0, n))],
        out_specs=pl.BlockSpec((1, MOD_ROWS, tn), lambda i, n: (i, 0, n)),
        out_shape=jax.ShapeDtypeStruct((depth, MOD_ROWS, p), F32),
        compiler_params=_params(("parallel", "parallel")),
        name="adaln_mod",
    )(cvec, ada_w, ada_b.reshape(depth, 1, p))


def _norm_proj_kernel(*refs, ctx_len, tm, has_extra):
    if has_extra:
        x_ref, nw_ref, scl_ref, shl_ref, scc_ref, shc_ref, w_ref, wx_ref, o_ref, ox_ref, h_ref = refs
    else:
        x_ref, nw_ref, scl_ref, shl_ref, scc_ref, shc_ref, w_ref, o_ref, h_ref = refs
    mi = pl.program_id(1)
    ni = pl.program_id(2)

    @pl.when(ni == 0)
    def _():
        x = x_ref[0]
        ms = jnp.mean(x * x, axis=-1, keepdims=True)
        y = x * lax.rsqrt(ms + RMS_EPS) * nw_ref[...]
        row = lax.broadcasted_iota(jnp.int32, (tm, 1), 0) + mi * tm
        is_ctx = row < ctx_len
        sc = jnp.where(is_ctx, scc_ref[...], scl_ref[0])
        sh = jnp.where(is_ctx, shc_ref[...], shl_ref[0])
        h = (y * (1.0 + sc) + sh).astype(BF16)
        h_ref[...] = h
        if has_extra:
            ox_ref[0] = _dot(h, wx_ref[...])

    o_ref[0] = _dot(h_ref[...], w_ref[...]).astype(BF16)


def _norm_proj(xa, nw, scl, shl, scc, shc, w, wx, ctx_len, tm):
    b, t, d = xa.shape
    p = w.shape[1]
    tn = PROJ_TN if p % PROJ_TN == 0 else PROJ_TN // 2
    assert t % tm == 0 and p % tn == 0
    has_extra = wx is not None
    vec_l = pl.BlockSpec((1, 1, d), lambda bi, mi, ni: (bi, 0, 0))
    vec_c = pl.BlockSpec((1, d), lambda bi, mi, ni: (0, 0))
    in_specs = [pl.BlockSpec((1, tm, d), lambda bi, mi, ni: (bi, mi, 0)),
                vec_c, vec_l, vec_l, vec_c, vec_c,
                pl.BlockSpec((d, tn), lambda bi, mi, ni: (0, ni))]
    out_specs = [pl.BlockSpec((1, tm, tn), lambda bi, mi, ni: (bi, mi, ni))]
    out_shape = [jax.ShapeDtypeStruct((b, t, p), BF16)]
    args = [xa, nw.reshape(1, d), scl, shl, scc, shc, w]
    if has_extra:
        px = wx.shape[1]
        in_specs.append(pl.BlockSpec((d, px), lambda bi, mi, ni: (0, 0)))
        out_specs.append(pl.BlockSpec((1, tm, px), lambda bi, mi, ni: (bi, mi, 0)))
        out_shape.append(jax.ShapeDtypeStruct((b, t, px), F32))
        args.append(wx)
    outs = pl.pallas_call(
        functools.partial(_norm_proj_kernel, ctx_len=ctx_len, tm=tm, has_extra=has_extra),
        grid=(b, t // tm, p // tn),
        in_specs=in_specs,
        out_specs=out_specs,
        out_shape=out_shape,
        scratch_shapes=[pltpu.VMEM((tm, d), BF16)],
        compiler_params=_params(("parallel", "parallel", "arbitrary")),
        name="norm_proj",
    )(*args)
    return (outs[0], outs[1]) if has_extra else (outs[0], None)


def _out_proj_kernel(y_ref, w_ref, x_ref, gtl_ref, gtc_ref, o_ref, *, ctx_len, tm):
    mi = pl.program_id(1)
    acc = _dot(y_ref[0], w_ref[...])
    row = lax.broadcasted_iota(jnp.int32, (tm, 1), 0) + mi * tm
    gt = jnp.where(row < ctx_len, gtc_ref[...], gtl_ref[0])
    o_ref[0] = x_ref[0] + gt * acc


def _out_proj(y, w, xa, gtl, gtc, ctx_len, tm):
    b, t, d = xa.shape
    kd = y.shape[2]
    tn = PROJ_TN if d % PROJ_TN == 0 else PROJ_TN // 2
    while 2 * (tm * kd * 2 + kd * tn * 2 + 2 * tm * tn * 4) > V7X_VMEM_LIMIT * 3 // 4:
        tn //= 2
    assert d % tn == 0 and t % tm == 0
    return pl.pallas_call(
        functools.partial(_out_proj_kernel, ctx_len=ctx_len, tm=tm),
        grid=(b, t // tm, d // tn),
        in_specs=[pl.BlockSpec((1, tm, kd), lambda bi, mi, ni: (bi, mi, 0)),
                  pl.BlockSpec((kd, tn), lambda bi, mi, ni: (0, ni)),
                  pl.BlockSpec((1, tm, tn), lambda bi, mi, ni: (bi, mi, ni)),
                  pl.BlockSpec((1, 1, tn), lambda bi, mi, ni: (bi, 0, ni)),
                  pl.BlockSpec((1, tn), lambda bi, mi, ni: (0, ni))],
        out_specs=pl.BlockSpec((1, tm, tn), lambda bi, mi, ni: (bi, mi, ni)),
        out_shape=jax.ShapeDtypeStruct((b, t, d), F32),
        compiler_params=_params(("parallel", "parallel", "arbitrary")),
        name="out_proj",
    )(y, w, xa, gtl, gtc)


def _final_norm_kernel(x_ref, w_ref, o_ref):
    x = x_ref[0]
    ms = jnp.mean(x * x, axis=-1, keepdims=True)
    o_ref[0] = x * lax.rsqrt(ms + RMS_EPS) * w_ref[...]


def _final_norm(xa, w, ctx_len):
    b, t, d = xa.shape
    n = t - ctx_len
    tr = math.gcd(ctx_len, 256)
    assert n % tr == 0
    off = ctx_len // tr
    return pl.pallas_call(
        _final_norm_kernel,
        grid=(b, n // tr),
        in_specs=[pl.BlockSpec((1, tr, d), lambda bi, ri: (bi, ri + off, 0)),
                  pl.BlockSpec((1, d), lambda bi, ri: (0, 0))],
        out_specs=pl.BlockSpec((1, tr, d), lambda bi, ri: (bi, ri, 0)),
        out_shape=jax.ShapeDtypeStruct((b, n, d), F32),
        compiler_params=_params(("parallel", "parallel")),
        name="final_norm",
    )(xa, w.reshape(1, d))


def _dn_conv_kernel(x_ref, w_ref, o_ref, *, ctx_len, n_qk_blocks):
    blk = pl.program_id(1)
    x = x_ref[0].astype(F32)
    t = x.shape[0]
    w = w_ref[...]
    row = lax.broadcasted_iota(jnp.int32, x.shape, 0)
    seg = row >= ctx_len
    half = DN_CONV_K // 2
    acc = x * w[half:half + 1, :]
    for tap in range(DN_CONV_K):
        dlt = tap - half
        if dlt == 0:
            continue
        shifted = pltpu.roll(x, (-dlt) % t, 0)
        src = row + dlt
        ok = (src >= 0) & (src < t) & ((src >= ctx_len) == seg)
        acc = acc + jnp.where(ok, shifted, 0.0) * w[tap:tap + 1, :]
    y = _silu(acc)
    inv = lax.rsqrt(jnp.sum(y * y, axis=-1, keepdims=True) + 1e-6)
    y = y * jnp.where(blk < n_qk_blocks, inv, 1.0)
    o_ref[0] = y.astype(BF16)


def _dn_conv(proj, conv_w_t, ctx_len, conv_ch, n_qk_blocks):
    b, t, _ = proj.shape
    return pl.pallas_call(
        functools.partial(_dn_conv_kernel, ctx_len=ctx_len, n_qk_blocks=n_qk_blocks),
        grid=(b, conv_ch // HEAD_DIM),
        in_specs=[pl.BlockSpec((1, t, HEAD_DIM), lambda bi, ci: (bi, 0, ci)),
                  pl.BlockSpec((DN_CONV_K, HEAD_DIM), lambda bi, ci: (0, ci))],
        out_specs=pl.BlockSpec((1, t, HEAD_DIM), lambda bi, ci: (bi, 0, ci)),
        out_shape=jax.ShapeDtypeStruct((b, t, conv_ch), BF16),
        compiler_params=_params(("parallel", "parallel")),
        name="dn_conv",
    )(proj, conv_w_t)


def _exact_f32_dot(tri, g):
    hi = g.astype(BF16)
    r1 = g - hi.astype(F32)
    mid = r1.astype(BF16)
    lo = (r1 - mid.astype(F32)).astype(BF16)
    return _dot(tri, hi) + _dot(tri, mid) + _dot(tri, lo)


def _dn_gate_kernel(ba_ref, alog_ref, dtb_ref, beta_ref, gc_ref, *, hv):
    ii = lax.broadcasted_iota(jnp.int32, (DN_CHUNK, DN_CHUNK), 0)
    jj = lax.broadcasted_iota(jnp.int32, (DN_CHUNK, DN_CHUNK), 1)
    lower = jnp.where(ii >= jj, 1.0, 0.0).astype(BF16)
    upper = jnp.where(ii <= jj, 1.0, 0.0).astype(BF16)
    ba = ba_ref[0]
    z = ba + dtb_ref[...]
    softplus = jnp.maximum(z, 0.0) + jnp.log1p(jnp.exp(-jnp.abs(z)))
    g = -jnp.exp(alog_ref[...]) * softplus
    beta = _sigmoid(ba)
    beta_ref[0, 0] = beta[:, 0:hv]
    beta_ref[0, 1] = beta[:, 2 * hv:3 * hv]
    for ci in range(ba.shape[0] // DN_CHUNK):
        rows = slice(ci * DN_CHUNK, (ci + 1) * DN_CHUNK)
        gc_ref[0, 0, rows, :] = _exact_f32_dot(lower, g[rows])[:, hv:2 * hv]
        gc_ref[0, 1, rows, :] = _exact_f32_dot(upper, g[rows])[:, 3 * hv:4 * hv]


def _dn_gates(ba, alog_row, dtb_row, hv):
    b, t, pw = ba.shape
    out = jax.ShapeDtypeStruct((b, 2, t, hv), F32)
    rows = next(r * DN_CHUNK for r in (4, 2, 1) if t % (r * DN_CHUNK) == 0)
    ospec = pl.BlockSpec((1, 2, rows, hv), lambda bi, ci: (bi, 0, ci, 0))
    return pl.pallas_call(
        functools.partial(_dn_gate_kernel, hv=hv),
        grid=(b, t // rows),
        in_specs=[pl.BlockSpec((1, rows, pw), lambda bi, ci: (bi, ci, 0)),
                  pl.BlockSpec((1, pw), lambda bi, ci: (0, 0)),
                  pl.BlockSpec((1, pw), lambda bi, ci: (0, 0))],
        out_specs=[ospec, ospec],
        out_shape=[out, out],
        compiler_params=_params(("parallel", "parallel")),
        name="dn_gates",
    )(ba, alog_row, dtb_row)


def _dn_chunk_kernel(q_ref, k_ref, v_ref, beta_ref, gcc_ref, gcr_ref, o_ref, s_ref, *, heads, scale):
    d = pl.program_id(2)
    p = pl.program_id(3)

    @pl.when(p == 0)
    def _():
        s_ref[...] = jnp.zeros_like(s_ref)

    c = DN_CHUNK
    ii = lax.broadcasted_iota(jnp.int32, (c, 2 * c), 0)
    lane = lax.broadcasted_iota(jnp.int32, (c, 2 * c), 1)
    jj = lane % c
    left = lane < c
    order = (ii - jj) * (1 - 2 * d)
    incl = order >= 0
    strict = order > 0
    eye = jnp.where(ii == jj, 1.0, 0.0)
    pair_masks = []
    sz = 1
    while sz < c:
        pair_masks.append((ii // (2 * sz) == jj // (2 * sz)) & (ii // sz != jj // sz))
        sz *= 2
    fwd = d == 0
    hr = range(heads)
    pr = range(heads // 2)
    lanes = lambda i: slice(i * HEAD_DIM, (i + 1) * HEAD_DIM)
    zeros_bf = jnp.zeros((c, HEAD_DIM), BF16)

    def block_diag(x):
        return jnp.concatenate([jnp.where(left, x, 0.0), jnp.where(left, 0.0, x)], axis=0).astype(BF16)

    def block_diag_wide(a, b):
        return jnp.concatenate([jnp.concatenate([a, zeros_bf], axis=1),
                                jnp.concatenate([zeros_bf, b], axis=1)], axis=0)

    kb, kq, kk, qk = [], [], [], []
    for m in pr:
        q = q_ref[0, :, lanes(m)]
        k = k_ref[0, :, lanes(m)]
        kq.append(jnp.concatenate([k, q], axis=0))
        both = _dot_nt(kq[m], jnp.concatenate([k, k], axis=0))
        kk.append(both[0:c])
        qk.append(both[c:2 * c] * scale)
        kb.append(k)
    beta = [beta_ref[0, 0, 0, :, h:h + 1] for h in hr]
    gcol = [gcc_ref[0, 0, 0, :, h:h + 1] for h in hr]
    beta2 = [jnp.where(left, beta[2 * m], beta[2 * m + 1]) for m in pr]
    gcol2 = [jnp.where(left, gcol[2 * m], gcol[2 * m + 1]) for m in pr]
    grow2 = [gcr_ref[0, 0, 0, 0, m:m + 1, :] for m in pr]
    gam = [jnp.where(incl, jnp.exp(jnp.where(incl, gcol2[m] - grow2[m], 0.0)), 0.0) for m in pr]
    low = [jnp.where(strict, beta2[m] * kk[m] * gam[m], 0.0) for m in pr]
    tinv = [eye - jnp.where(pair_masks[0], low[m], 0.0) for m in pr]
    for pm in pair_masks[1:]:
        xc = [_dot(tinv[m].astype(BF16), block_diag(jnp.where(pm, low[m], 0.0))) for m in pr]
        tinv = [tinv[m] - _dot(xc[m].astype(BF16), block_diag(tinv[m])) for m in pr]
    tb = [tinv[m].astype(BF16) for m in pr]
    eg = [jnp.exp(gcol[h]) for h in hr]
    glast = [jnp.where(fwd, gcol[h][c - 1:c, :], gcol[h][0:1, :]) for h in hr]
    s_old = [s_ref[h] for h in hr]
    half = lambda h: lanes(h % 2)
    kqs = [_dot(kq[m], jnp.concatenate([s_old[2 * m].astype(BF16), s_old[2 * m + 1].astype(BF16)], axis=1))
           for m in pr]
    rhs = [(beta[h] * (v_ref[0, :, lanes(h)].astype(F32) - eg[h] * kqs[h // 2][0:c, half(h)])).astype(BF16)
           for h in hr]
    v_new = [_dot(tb[m], block_diag_wide(rhs[2 * m], rhs[2 * m + 1])) for m in pr]
    vnb = [v_new[m].astype(BF16) for m in pr]
    for m in pr:
        intra = _dot((qk[m] * gam[m]).astype(BF16),
                     block_diag_wide(vnb[m][:, 0:HEAD_DIM], vnb[m][:, HEAD_DIM:2 * HEAD_DIM]))
        inter = jnp.concatenate([(scale * eg[h]) * kqs[m][c:2 * c, half(h)] for h in (2 * m, 2 * m + 1)], axis=1)
        o_ref[0, 0, :, 2 * m * HEAD_DIM:(2 * m + 2) * HEAD_DIM] = (inter + intra).astype(BF16)
    wv = [jnp.concatenate([jnp.exp(glast[h] - gcol[h]) * v_new[m][:, half(h)] for h in (2 * m, 2 * m + 1)],
                          axis=1).astype(BF16) for m in pr]
    upd = [_dot_tn(kb[m], wv[m]) for m in pr]
    for h in hr:
        s_ref[h] = s_old[h] * jnp.exp(glast[h]) + upd[h // 2][:, half(h)]


def _dn_chunks(qkv, beta_g, gcc_g, gcr_g, ctx_len, qk_w, hv):
    b, t, _ = qkv.shape
    g = beta_g.shape[-1]
    assert hv % g == 0 and qk_w % (g // 2 * HEAD_DIM) == 0
    nc = t // DN_CHUNK
    nctx = ctx_len // DN_CHUNK
    qw = g // 2 * HEAD_DIM
    vw = g * HEAD_DIM
    k_off = qk_w // qw
    v_off = 2 * qk_w // vw

    def chunk(d, p):
        back = jnp.where(p < nctx, nctx - 1 - p, nc + nctx - 1 - p)
        return jnp.where(d == 0, p, back)

    return pl.pallas_call(
        functools.partial(_dn_chunk_kernel, heads=g, scale=HEAD_DIM ** -0.5),
        grid=(b, hv // g, 2, nc),
        in_specs=[pl.BlockSpec((1, DN_CHUNK, qw), lambda bi, gi, d, p: (bi, chunk(d, p), gi)),
                  pl.BlockSpec((1, DN_CHUNK, qw), lambda bi, gi, d, p: (bi, chunk(d, p), k_off + gi)),
                  pl.BlockSpec((1, DN_CHUNK, vw), lambda bi, gi, d, p: (bi, chunk(d, p), v_off + gi)),
                  pl.BlockSpec((1, 1, 1, DN_CHUNK, g), lambda bi, gi, d, p: (bi, d, gi, chunk(d, p), 0)),
                  pl.BlockSpec((1, 1, 1, DN_CHUNK, g), lambda bi, gi, d, p: (bi, d, gi, chunk(d, p), 0)),
                  pl.BlockSpec((1, 1, 1, 1, g // 2, 2 * DN_CHUNK),
                               lambda bi, gi, d, p: (bi, d, gi, chunk(d, p), 0, 0))],
        out_specs=pl.BlockSpec((1, 1, DN_CHUNK, vw), lambda bi, gi, d, p: (bi, d, chunk(d, p), gi)),
        out_shape=jax.ShapeDtypeStruct((b, 2, t, hv * HEAD_DIM), BF16),
        scratch_shapes=[pltpu.VMEM((g, HEAD_DIM, HEAD_DIM), F32)],
        compiler_params=_params(("parallel", "parallel", "arbitrary", "arbitrary")),
        name="dn_chunks",
    )(qkv, qkv, qkv, beta_g, gcc_g, gcr_g)


def _dn_finish_kernel(o_ref, z_ref, w_ref, y_ref):
    cw = y_ref.shape[2]
    for h in range(cw // HEAD_DIM):
        hl = slice(h * HEAD_DIM, (h + 1) * HEAD_DIM)
        o = o_ref[0, 0, :, hl].astype(F32) + o_ref[0, 1, :, hl].astype(F32)
        ms = jnp.mean(o * o, axis=-1, keepdims=True)
        y = o * lax.rsqrt(ms + RMS_EPS) * w_ref[...]
        y_ref[0, :, hl] = (y * _silu(z_ref[0, :, hl].astype(F32))).astype(BF16)


def _dn_finish(o, proj, onorm_w, z_off):
    b, _, t, vw = o.shape
    cw = vw
    assert z_off % cw == 0
    tr = math.gcd(t, 128)
    return pl.pallas_call(
        _dn_finish_kernel,
        grid=(b, t // tr, vw // cw),
        in_specs=[pl.BlockSpec((1, 2, tr, cw), lambda bi, ri, ci: (bi, 0, ri, ci)),
                  pl.BlockSpec((1, tr, cw), lambda bi, ri, ci: (bi, ri, z_off // cw + ci)),
                  pl.BlockSpec((1, HEAD_DIM), lambda bi, ri, ci: (0, 0))],
        out_specs=pl.BlockSpec((1, tr, cw), lambda bi, ri, ci: (bi, ri, ci)),
        out_shape=jax.ShapeDtypeStruct((b, t, vw), BF16),
        compiler_params=_params(("parallel", "parallel", "parallel")),
        name="dn_finish",
    )(o, proj, onorm_w.reshape(1, HEAD_DIM))


def _deltanet(proj, ba, conv_w, a_log, dt_bias, onorm_w, ctx_len, d_model):
    hk = d_model // HEAD_DIM
    hv = 2 * hk
    qk_w = hk * HEAD_DIM
    conv_ch = 2 * qk_w + hv * HEAD_DIM
    b, t, _ = proj.shape
    g = min(DN_HEADS_PER_STEP, hv)
    qkv = _dn_conv(proj, jnp.transpose(conv_w), ctx_len, conv_ch, 2 * hk)
    zeros = jnp.zeros((hv,), F32)
    alog_row = jnp.concatenate([zeros, a_log[0], zeros, a_log[1]]).reshape(1, 4 * hv)
    dtb_row = jnp.concatenate([zeros, dt_bias[0], zeros, dt_bias[1]]).reshape(1, 4 * hv)
    beta, gc = _dn_gates(ba, alog_row, dtb_row, hv)
    nc = t // DN_CHUNK
    beta_g = beta.reshape(b, 2, t, hv // g, g).transpose(0, 1, 3, 2, 4)
    gcc_g = gc.reshape(b, 2, t, hv // g, g).transpose(0, 1, 3, 2, 4)
    gcr_g = (gc.reshape(b, 2, nc, DN_CHUNK, hv // g, g).transpose(0, 1, 4, 2, 5, 3)
             .reshape(b, 2, hv // g, nc, g // 2, 2 * DN_CHUNK))
    o = _dn_chunks(qkv, beta_g, gcc_g, gcr_g, ctx_len, qk_w, hv)
    return _dn_finish(o, proj, onorm_w, conv_ch)


def _rope(x, cos, sin):
    quarter = HEAD_DIM // 4
    lane = lax.broadcasted_iota(jnp.int32, x.shape, 1)
    up = pltpu.roll(x, HEAD_DIM - quarter, 1)
    down = pltpu.roll(x, quarter, 1)
    rot = jnp.where((lane % (2 * quarter)) < quarter, -up, down)
    return x * cos + rot * sin


def _da_kernel(q_ref, k_ref, v_ref, g_ref, cos_ref, sin_ref, lam_ref, sw_ref, o_ref,
               krot_ref, q_scr, m_ref, l_ref, acc_ref, *, ctx_len, lambda_init, scale, tq, ck):
    qi = pl.program_id(2)
    t = k_ref.shape[1]
    maps = range(2)
    lanes = lambda j: slice(j * HEAD_DIM, (j + 1) * HEAD_DIM)
    qscale = scale * math.log2(math.e)
    lam = lam_ref[...]
    lam_full = (jnp.exp(jnp.sum(lam[0:1] * lam[1:2], axis=-1, keepdims=True))
                - jnp.exp(jnp.sum(lam[2:3] * lam[3:4], axis=-1, keepdims=True)) + lambda_init)

    def finish(o, gate):
        ms = jnp.mean(o * o, axis=-1, keepdims=True)
        y = o * lax.rsqrt(ms + DA_SUBLN_EPS) * sw_ref[...] * (1.0 - lambda_init)
        return (y * _silu(gate.astype(F32))).astype(BF16)

    @pl.when(qi == 0)
    def _():
        for j in maps:
            kk = k_ref[0, :, lanes(j)].astype(F32)
            krot_ref[:, lanes(j)] = _rope(kk, cos_ref[...], sin_ref[...]).astype(BF16)
        outs = []
        for j in maps:
            q = _rope(q_ref[0, 0:ctx_len, lanes(j)].astype(F32), cos_ref[0:ctx_len, :], sin_ref[0:ctx_len, :])
            s = _dot_nt((q * qscale).astype(BF16), krot_ref[0:ctx_len, lanes(j)])
            pr = jnp.exp2(s - jnp.max(s, axis=-1, keepdims=True))
            outs.append(_dot(pr.astype(BF16), v_ref[0, 0:ctx_len, :]) / jnp.sum(pr, axis=-1, keepdims=True))
        o_ref[0, 0:ctx_len, :] = finish(outs[0] - lam_full * outs[1], g_ref[0, 0:ctx_len, :])

    @pl.when(qi > 0)
    def _():
        r0 = pl.multiple_of(ctx_len + (qi - 1) * tq, math.gcd(ctx_len, tq))
        rows = pl.ds(r0, tq)
        for j in maps:
            q = _rope(q_ref[0, rows, lanes(j)].astype(F32), cos_ref[rows, :], sin_ref[rows, :])
            q_scr[j] = (q * qscale).astype(BF16)

        stat = (tq, HEAD_DIM)

        def widen(x, width):
            if width % HEAD_DIM == 0:
                return jnp.tile(x, (1, width // HEAD_DIM))
            return jnp.broadcast_to(x[:, 0:1], (tq, width))

        for j in maps:
            s = _dot_nt(q_scr[j], krot_ref[0:ctx_len, lanes(j)])
            m = jnp.broadcast_to(jnp.max(s, axis=-1, keepdims=True), stat)
            pr = jnp.exp2(s - widen(m, ctx_len))
            m_ref[j] = m
            l_ref[j] = jnp.broadcast_to(jnp.sum(pr, axis=-1, keepdims=True), stat)
            acc_ref[j] = _dot(pr.astype(BF16), v_ref[0, 0:ctx_len, :])

        def body(c, carry):
            k0 = pl.multiple_of(ctx_len + c * ck, math.gcd(ctx_len, ck))
            s = [_dot_nt(q_scr[j], krot_ref[pl.ds(k0, ck), lanes(j)]) for j in maps]
            m_old = [m_ref[j] for j in maps]
            m_new = [jnp.maximum(m_old[j], jnp.broadcast_to(jnp.max(s[j], axis=-1, keepdims=True), stat))
                     for j in maps]
            alpha = [jnp.exp2(m_old[j] - m_new[j]) for j in maps]
            pr = [jnp.exp2(s[j] - widen(m_new[j], ck)) for j in maps]
            v = v_ref[0, pl.ds(k0, ck), :]
            for j in maps:
                m_ref[j] = m_new[j]
                l_ref[j] = alpha[j] * l_ref[j] + jnp.broadcast_to(jnp.sum(pr[j], axis=-1, keepdims=True), stat)
                acc_ref[j] = widen(alpha[j], 2 * HEAD_DIM) * acc_ref[j] + _dot(pr[j].astype(BF16), v)
            return carry

        lax.fori_loop(0, (t - ctx_len) // ck, body, 0)
        o = (acc_ref[0] / widen(l_ref[0], 2 * HEAD_DIM)
             - lam_full * (acc_ref[1] / widen(l_ref[1], 2 * HEAD_DIM)))
        o_ref[0, rows, :] = finish(o, g_ref[0, rows, :])


def _rope_tables(ctx_len, n):
    rows = n // GRID_W
    half = HEAD_DIM // 2
    inv = ROPE_THETA ** (-jnp.arange(0, half, 2, dtype=F32) / half)
    ang_r = jnp.arange(rows, dtype=F32)[:, None] * inv
    ang_c = jnp.arange(GRID_W, dtype=F32)[:, None] * inv

    def table(fn, ctx_value):
        fr = jnp.broadcast_to(fn(ang_r)[:, None, :], (rows, GRID_W, half // 2))
        fc = jnp.broadcast_to(fn(ang_c)[None, :, :], (rows, GRID_W, half // 2))
        lat = jnp.concatenate([fr, fr, fc, fc], axis=-1).reshape(n, HEAD_DIM)
        return jnp.concatenate([jnp.full((ctx_len, HEAD_DIM), ctx_value, F32), lat], axis=0)

    return table(jnp.cos, 1.0), table(jnp.sin, 0.0)


def _diff_attn(proj, lam, subln_w, ctx_len, d_model, layer_idx):
    b, t, _ = proj.shape
    hw = 2 * HEAD_DIM
    heads = d_model // hw
    n = t - ctx_len
    tq = min(DA_QUERY_TILE, n)
    ck = min(DA_KEY_CHUNK, n)
    assert n % tq == 0 and n % ck == 0
    lambda_init = 0.8 - 0.6 * math.exp(-0.3 * layer_idx)
    cos, sin = _rope_tables(ctx_len, n)
    full = lambda bi, hi, qi: (0, 0)
    once = pl.Buffered(1)
    col = lambda off: (lambda bi, hi, qi: (bi, 0, off * heads + hi))
    return pl.pallas_call(
        functools.partial(_da_kernel, ctx_len=ctx_len, lambda_init=lambda_init, scale=HEAD_DIM ** -0.5,
                          tq=tq, ck=ck),
        grid=(b, heads, 1 + n // tq),
        in_specs=[pl.BlockSpec((1, t, hw), col(0)),
                  pl.BlockSpec((1, t, hw), col(1)),
                  pl.BlockSpec((1, t, hw), col(2)),
                  pl.BlockSpec((1, t, hw), col(3)),
                  pl.BlockSpec((t, HEAD_DIM), full, pipeline_mode=once),
                  pl.BlockSpec((t, HEAD_DIM), full, pipeline_mode=once),
                  pl.BlockSpec((4, HEAD_DIM), full),
                  pl.BlockSpec((1, hw), full)],
        out_specs=pl.BlockSpec((1, t, hw), col(0)),
        out_shape=jax.ShapeDtypeStruct((b, t, heads * hw), BF16),
        scratch_shapes=[pltpu.VMEM((t, hw), BF16),
                        pltpu.VMEM((2, tq, HEAD_DIM), BF16),
                        pltpu.VMEM((2, tq, HEAD_DIM), F32),
                        pltpu.VMEM((2, tq, HEAD_DIM), F32),
                        pltpu.VMEM((2, tq, hw), F32)],
        compiler_params=_params(("parallel", "parallel", "arbitrary")),
        name="diff_attn",
    )(proj, proj, proj, proj, cos, sin, lam, subln_w.reshape(1, hw))


def _na_kernel(q_ref, k_ref, v_ref, g_ref, bias_ref, o_ref, *, ctx_len, rows, scale):
    wk = NA_WR * GRID_W
    kc = k_ref[0, 0:ctx_len, :]
    vc = v_ref[0, 0:ctx_len, :]

    s = _dot_nt(q_ref[0, 0:ctx_len, :], kc) * scale
    m = jnp.max(s, axis=-1, keepdims=True)
    pr = jnp.exp(s - m)
    o = _dot(pr.astype(BF16), vc) / jnp.sum(pr, axis=-1, keepdims=True)
    o_ref[0, 0:ctx_len, :] = (o * _silu(g_ref[0, 0:ctx_len, :].astype(F32))).astype(BF16)

    per_iter = math.gcd(rows, NA_ROWS_PER_ITER)

    def body(it, carry):
        nr = range(per_iter)
        r = [it * per_iter + i for i in nr]
        rs = [jnp.clip(r[i] - NA_WR // 2, 0, rows - NA_WR) for i in nr]
        qs = [pl.multiple_of(ctx_len + r[i] * GRID_W, GRID_W) for i in nr]
        ks = [pl.multiple_of(ctx_len + rs[i] * GRID_W, GRID_W) for i in nr]
        q = [q_ref[0, pl.ds(qs[i], GRID_W), :] for i in nr]
        s_win = [_dot_nt(q[i], k_ref[0, pl.ds(ks[i], wk), :]) * scale + bias_ref[0, rs[i] - r[i] + NA_WR - 1]
                 for i in nr]
        s_ctx = [_dot_nt(q[i], kc) * scale for i in nr]
        mx = [jnp.maximum(jnp.max(s_win[i], axis=-1, keepdims=True), jnp.max(s_ctx[i], axis=-1, keepdims=True))
              for i in nr]
        p_win = [jnp.exp(s_win[i] - mx[i]) for i in nr]
        p_ctx = [jnp.exp(s_ctx[i] - mx[i]) for i in nr]
        den = [jnp.sum(p_win[i], axis=-1, keepdims=True) + jnp.sum(p_ctx[i], axis=-1, keepdims=True) for i in nr]
        out = [(_dot(p_win[i].astype(BF16), v_ref[0, pl.ds(ks[i], wk), :]) + _dot(p_ctx[i].astype(BF16), vc)) / den[i]
               for i in nr]
        for i in nr:
            gate = g_ref[0, pl.ds(qs[i], GRID_W), :].astype(F32)
            o_ref[0, pl.ds(qs[i], GRID_W), :] = (out[i] * _silu(gate)).astype(BF16)
        return carry

    lax.fori_loop(0, rows // per_iter, body, 0)


def _na_bias_table(rpb):
    heads = rpb.shape[0]
    cols = jnp.arange(GRID_W)
    start = jnp.clip(cols - NA_WC // 2, 0, GRID_W - NA_WC)
    inside = (cols[None, :] >= start[:, None]) & (cols[None, :] < start[:, None] + NA_WC)
    dc = cols[None, :] - cols[:, None] + NA_WC - 1
    rpb = rpb.astype(F32)
    wide = jnp.full((heads, 2 * NA_WR - 1, GRID_W, GRID_W), NEG_BIG, F32)
    for off in range(2 * NA_WC - 1):
        wide = jnp.where((inside & (dc == off))[None, None], rpb[:, :, off, None, None], wide)
    tab = jnp.stack([wide[:, o:o + NA_WR] for o in range(NA_WR)], axis=1)
    return tab.transpose(0, 1, 3, 2, 4).reshape(heads, NA_WR, GRID_W, NA_WR * GRID_W)


def _neighbourhood(proj, rpb, ctx_len, d_model):
    b, t, _ = proj.shape
    heads = d_model // HEAD_DIM
    rows = (t - ctx_len) // GRID_W
    assert rows >= NA_WR
    wk = NA_WR * GRID_W
    bias = _na_bias_table(rpb)
    col = lambda off: (lambda bi, hi: (bi, 0, off * heads + hi))
    return pl.pallas_call(
        functools.partial(_na_kernel, ctx_len=ctx_len, rows=rows, scale=HEAD_DIM ** -0.5),
        grid=(b, heads),
        in_specs=[pl.BlockSpec((1, t, HEAD_DIM), col(0)),
                  pl.BlockSpec((1, t, HEAD_DIM), col(1)),
                  pl.BlockSpec((1, t, HEAD_DIM), col(2)),
                  pl.BlockSpec((1, t, HEAD_DIM), col(3)),
                  pl.BlockSpec((1, NA_WR, GRID_W, wk), lambda bi, hi: (hi, 0, 0, 0))],
        out_specs=pl.BlockSpec((1, t, HEAD_DIM), col(0)),
        out_shape=jax.ShapeDtypeStruct((b, t, heads * HEAD_DIM), BF16),
        compiler_params=_params(("parallel", "parallel")),
        name="neighbourhood",
    )(proj, proj, proj, proj, bias)


def kernel(x, c, ctx, c_ctx, norm_w, ada_w, ada_b, dn_w_in, dn_conv_w, dn_a_log, dn_dt_bias, dn_onorm_w,
           dn_w_out, da_w_in, da_lambda, da_subln_w, da_w_out, na_w_in, na_rpb, na_w_out, final_norm_w):
    b, n, d = x.shape
    ctx_len = ctx.shape[1]
    t = ctx_len + n
    depth = norm_w.shape[0]
    assert b + 1 <= MOD_ROWS and n % GRID_W == 0 and ctx_len % DN_CHUNK == 0
    tm = t // 4
    assert t % 4 == 0 and tm % 16 == 0

    xa = jnp.concatenate([ctx, x], axis=1)
    cvec = jnp.concatenate([c, c_ctx[None, :], jnp.zeros((MOD_ROWS - b - 1, d), F32)], axis=0)
    mods = _modulation(cvec, ada_w, ada_b)

    conv_ch = 4 * d
    for i in range(depth):
        kind, j = i % N_MIXERS, i // N_MIXERS
        sh, sc, gt = mods[i, :, 0:d], mods[i, :, d:2 * d], mods[i, :, 2 * d:3 * d]
        lat = lambda m: m[0:b, None, :]
        cx = lambda m: m[b:b + 1, :]
        if kind == 0:
            w_in = dn_w_in[j]
            proj, ba = _norm_proj(xa, norm_w[i], lat(sc), lat(sh), cx(sc), cx(sh),
                                  w_in[:, 0:conv_ch + 2 * d].astype(BF16), w_in[:, conv_ch + 2 * d:].astype(BF16),
                                  ctx_len, tm)
            y = _deltanet(proj, ba, dn_conv_w[j], dn_a_log[j], dn_dt_bias[j], dn_onorm_w[j], ctx_len, d)
            w_out = dn_w_out[j]
        elif kind == 1:
            proj, _ = _norm_proj(xa, norm_w[i], lat(sc), lat(sh), cx(sc), cx(sh),
                                 da_w_in[j].astype(BF16), None, ctx_len, tm)
            y = _diff_attn(proj, da_lambda[j], da_subln_w[j], ctx_len, d, i)
            w_out = da_w_out[j]
        else:
            proj, _ = _norm_proj(xa, norm_w[i], lat(sc), lat(sh), cx(sc), cx(sh),
                                 na_w_in[j].astype(BF16), None, ctx_len, tm)
            y = _neighbourhood(proj, na_rpb[j], ctx_len, d)
            w_out = na_w_out[j]
        xa = _out_proj(y, w_out.astype(BF16), xa, lat(gt), cx(gt), ctx_len, tm)
    return _final_norm(xa, final_norm_w, ctx_len)
```
